```python
import jax, jax.numpy as jnp
from jax import lax
import numpy as np

D_MODEL = 2048
BATCH = 2
SEQ = 16384
DEPTH = 2

N_META = 16
BLOCK = 128
PAD_FRONT = BLOCK - N_META
SWA_HEADS = 16
SWA_KV_HEADS = 2
SWA_HEAD_DIM = 64
SWA_GROUP = SWA_HEADS // SWA_KV_HEADS
WINDOW = 128
MLA_HEADS = 8
MLA_Q_RANK = 768
MLA_KV_RANK = 512
MLA_NOPE_DIM = 128
MLA_ROPE_DIM = 64
MLA_V_DIM = 128
MLA_QK_DIM = MLA_NOPE_DIM + MLA_ROPE_DIM
ROPE_THETA = 10000.0
SWA_Q_W = SWA_HEADS * SWA_HEAD_DIM
SWA_KV_W = SWA_KV_HEADS * SWA_HEAD_DIM
IN_SPLITS = (SWA_Q_W,
             SWA_Q_W + SWA_KV_W,
             SWA_Q_W + 2 * SWA_KV_W,
             SWA_Q_W + 2 * SWA_KV_W + MLA_Q_RANK,
             SWA_Q_W + 2 * SWA_KV_W + MLA_Q_RANK + MLA_KV_RANK)
IN_WIDTH = SWA_Q_W + 2 * SWA_KV_W + MLA_Q_RANK + MLA_KV_RANK + MLA_ROPE_DIM
MIX_WIDTH = SWA_Q_W + MLA_HEADS * MLA_V_DIM
CONV_CH = D_MODEL
CONV_WIDTH = 31
D_FF = 7 * D_MODEL // 2
N_EXPERTS = 8
TOP_K = 2
MOE_BLOCK = 256
DN_ALPHA = (2 * DEPTH) ** 0.25
DN_BETA = (8 * DEPTH) ** -0.25
N_EVEN = (DEPTH + 1) // 2
N_ODD = DEPTH // 2
NEG_INF = -1e30
LN_EPS = 1e-5
RMS_EPS = 1e-6
F32 = jnp.float32

kernel_name = "hybrid_swa_mla_conformer_moe_deepnorm"


def layer_norm(x, g, b):
    xf = x.astype(F32)
    mu = jnp.mean(xf, -1, keepdims=True)
    xc = xf - mu
    var = jnp.mean(xc * xc, -1, keepdims=True)
    return (xc * lax.rsqrt(var + LN_EPS) * g.astype(F32) + b.astype(F32)).astype(x.dtype)


def rms_norm(x, g):
    xf = x.astype(F32)
    return (xf * lax.rsqrt(jnp.mean(xf * xf, -1, keepdims=True) + RMS_EPS) * g.astype(F32)).astype(x.dtype)


def rope(t, pos):
    d = t.shape[-1]
    half = d // 2
    inv = ROPE_THETA ** (-jnp.arange(half, dtype=F32) * 2.0 / d)
    ang = pos.astype(F32)[:, None] * inv[None, :]
    cos = jnp.cos(ang)[:, None, :]
    sin = jnp.sin(ang)[:, None, :]
    tf = t.astype(F32)
    t1, t2 = tf[..., :half], tf[..., half:]
    return jnp.concatenate([t1 * cos - t2 * sin, t2 * cos + t1 * sin], -1).astype(t.dtype)


def swa_sink_attention(q, k, v, sinks):
    B, Lp = q.shape[0], q.shape[1]
    nb = Lp // BLOCK
    qb = q.reshape(B, nb, BLOCK, SWA_KV_HEADS, SWA_GROUP, SWA_HEAD_DIM)
    kb = k.reshape(B, nb, BLOCK, SWA_KV_HEADS, SWA_HEAD_DIM)
    vb = v.reshape(B, nb, BLOCK, SWA_KV_HEADS, SWA_HEAD_DIM)
    zk = jnp.zeros_like(kb[:, :1])
    k_band = jnp.concatenate([jnp.concatenate([zk, kb[:, :-1]], 1), kb], 2)
    v_band = jnp.concatenate([jnp.concatenate([zk, vb[:, :-1]], 1), vb], 2)
    k_meta = k[:, PAD_FRONT:BLOCK]
    v_meta = v[:, PAD_FRONT:BLOCK]
    scale = SWA_HEAD_DIM ** -0.5
    s_band = jnp.einsum('bnqhgd,bnkhd->bnhgqk', qb, k_band, preferred_element_type=F32) * scale
    s_meta = jnp.einsum('bnqhgd,bmhd->bnhgqm', qb, k_meta, preferred_element_type=F32) * scale
    blk = jnp.arange(nb)[:, None]
    q_pos = blk * BLOCK + jnp.arange(BLOCK)[None, :] - PAD_FRONT
    k_pos = blk * BLOCK + jnp.arange(2 * BLOCK)[None, :] - BLOCK - PAD_FRONT
    kp = k_pos[:, None, :]
    qp = q_pos[:, :, None]
    band_ok = (kp >= N_META) & (kp <= qp) & (kp > qp - WINDOW)
    meta_ok = jnp.arange(N_META)[None, None, :] <= qp
    s_band = jnp.where(band_ok[None, :, None, None], s_band, NEG_INF)
    s_meta = jnp.where(meta_ok[None, :, None, None], s_meta, NEG_INF)
    sink = jnp.broadcast_to(sinks.astype(F32).reshape(1, 1, SWA_KV_HEADS, SWA_GROUP, 1, 1),
                            s_meta.shape[:-1] + (1,))
    p = jax.nn.softmax(jnp.concatenate([sink, s_meta, s_band], -1), axis=-1)
    p_meta = p[..., 1:1 + N_META].astype(v.dtype)
    p_band = p[..., 1 + N_META:].astype(v.dtype)
    out = (jnp.einsum('bnhgqm,bmhd->bnqhgd', p_meta, v_meta)
           + jnp.einsum('bnhgqk,bnkhd->bnqhgd', p_band, v_band))
    return out.reshape(B, Lp, SWA_Q_W)


def mla_attention(q, k, v):
    B, Lp = q.shape[0], q.shape[1]
    nb = Lp // BLOCK
    scale = MLA_QK_DIM ** -0.5
    k_pos = jnp.arange(Lp) - PAD_FRONT
    qb = jnp.moveaxis(q.reshape(B, nb, BLOCK, MLA_HEADS, MLA_QK_DIM), 1, 0)

    def one_block(args):
        q_blk, n = args
        q_pos = n * BLOCK + jnp.arange(BLOCK) - PAD_FRONT
        s = jnp.einsum('bqhd,bkhd->bhqk', q_blk, k, preferred_element_type=F32) * scale
        ok = (k_pos[None, :] >= 0) & (k_pos[None, :] <= q_pos[:, None])
        p = jax.nn.softmax(jnp.where(ok, s, NEG_INF), axis=-1).astype(v.dtype)
        return jnp.einsum('bhqk,bkhd->bqhd', p, v)

    out = lax.map(one_block, (qb, jnp.arange(nb)))
    return jnp.moveaxis(out, 0, 1).reshape(B, Lp, MLA_HEADS * MLA_V_DIM)


def parallel_attention_mixer(x, w_in, sinks, g_cq, w_uq, g_ckv, w_ukv, w_o):
    B, L, _ = x.shape
    h = x @ w_in
    a_q, a_k, a_v, c_q, c_kv, k_r = jnp.split(h, IN_SPLITS, axis=-1)
    a_q = a_q.reshape(B, L, SWA_HEADS, SWA_HEAD_DIM)
    a_k = a_k.reshape(B, L, SWA_KV_HEADS, SWA_HEAD_DIM)
    a_v = a_v.reshape(B, L, SWA_KV_HEADS, SWA_HEAD_DIM)
    pos = jnp.arange(L)
    q = (rms_norm(c_q, g_cq) @ w_uq).reshape(B, L, MLA_HEADS, MLA_QK_DIM)
    q = jnp.concatenate([q[..., :MLA_NOPE_DIM], rope(q[..., MLA_NOPE_DIM:], pos)], -1)
    kv = (rms_norm(c_kv, g_ckv) @ w_ukv).reshape(B, L, MLA_HEADS, MLA_NOPE_DIM + MLA_V_DIM)
    k_rope = rope(k_r[:, :, None, :], pos)
    k = jnp.concatenate([kv[..., :MLA_NOPE_DIM],
                         jnp.broadcast_to(k_rope, (B, L, MLA_HEADS, MLA_ROPE_DIM))], -1)
    v = kv[..., MLA_NOPE_DIM:]

    def pad(t):
        return jnp.pad(t, ((0, 0), (PAD_FRONT, 0), (0, 0), (0, 0)))

    out_a = swa_sink_attention(pad(a_q), pad(a_k), pad(a_v), sinks)
    out_b = mla_attention(pad(q), pad(k), pad(v))
    out = jnp.concatenate([out_a, out_b], -1)[:, PAD_FRONT:]
    return out @ w_o


def conformer_conv_module(x, w_pw1, b_pw1, w_dw, b_dw, g_cn, b_cn, w_pw2, b_pw2):
    h = x @ w_pw1 + b_pw1
    a, gate = jnp.split(h, 2, axis=-1)
    u = a * jax.nn.sigmoid(gate)
    u = lax.conv_general_dilated(u, w_dw[:, None, :], window_strides=(1,),
                                 padding=((CONV_WIDTH - 1, 0),),
                                 dimension_numbers=('NWC', 'WIO', 'NWC'),
                                 feature_group_count=CONV_CH) + b_dw
    u = jax.nn.silu(layer_norm(u, g_cn, b_cn))
    return u @ w_pw2 + b_pw2


def swiglu(x, w_gate, w_up, w_down):
    return (jax.nn.silu(x @ w_gate) * (x @ w_up)) @ w_down


def moe_swiglu(t, w_router, b_router, w_gate, w_up, w_down):
    T, D = t.shape
    logits = jnp.matmul(t, w_router, preferred_element_type=F32) + b_router.astype(F32)
    top_logits, top_idx = lax.top_k(logits, TOP_K)
    gates = jax.nn.softmax(top_logits, axis=-1)
    A = T * TOP_K
    flat_e = top_idx.reshape(A)
    flat_tok = jnp.repeat(jnp.arange(T, dtype=jnp.int32), TOP_K)
    flat_g = gates.reshape(A)
    order = jnp.argsort(flat_e)
    se, stok, sg = flat_e[order], flat_tok[order], flat_g[order]
    counts = jnp.bincount(flat_e, length=N_EXPERTS)
    padded = (counts + MOE_BLOCK - 1) // MOE_BLOCK * MOE_BLOCK
    start = jnp.cumsum(counts) - counts
    pend = jnp.cumsum(padded)
    pstart = pend - padded
    dest = pstart[se] + jnp.arange(A) - start[se]
    nblk = -(-A // MOE_BLOCK) + N_EXPERTS
    P = nblk * MOE_BLOCK
    buf_tok = jnp.zeros((P,), jnp.int32).at[dest].set(stok)
    buf_g = jnp.zeros((P,), F32).at[dest].set(sg)
    blk_e = jnp.minimum(jnp.sum(jnp.arange(nblk)[:, None] * MOE_BLOCK >= pend[None, :], -1),
                        N_EXPERTS - 1)
    xb = t[buf_tok].reshape(nblk, MOE_BLOCK, D)

    def expert_block(args):
        xblk, e = args
        return swiglu(xblk, w_gate[e], w_up[e], w_down[e])

    yb = lax.map(expert_block, (xb, blk_e)).reshape(P, D)
    y = jnp.zeros((T, D), F32).at[buf_tok].add(yb.astype(F32) * buf_g[:, None])
    return y.astype(t.dtype)


def setup_inputs(seed: int = 0) -> dict:
    key = jax.random.key(seed)
    ks = iter(jax.random.split(key, 40))

    def nrm(shape, scale):
        return jax.random.normal(next(ks), shape, F32) * scale

    def gain(shape):
        return 1.0 + nrm(shape, 0.02)

    x = nrm((BATCH, SEQ, D_MODEL), 1.0)
    meta_tokens = nrm((N_META, D_MODEL), 1.0)
    col_scale = jnp.concatenate([jnp.ones((SWA_Q_W + SWA_KV_W,), F32),
                                 jnp.full((SWA_KV_W,), DN_BETA, F32),
                                 jnp.ones((MLA_Q_RANK + MLA_KV_RANK + MLA_ROPE_DIM,), F32)])
    ev_w_in = nrm((N_EVEN, D_MODEL, IN_WIDTH), D_MODEL ** -0.5) * col_scale
    ev_sinks = nrm((N_EVEN, SWA_HEADS), 0.5)
    ev_g_cq = gain((N_EVEN, MLA_Q_RANK))
    ev_w_uq = nrm((N_EVEN, MLA_Q_RANK, MLA_HEADS * MLA_QK_DIM), MLA_Q_RANK ** -0.5)
    ev_g_ckv = gain((N_EVEN, MLA_KV_RANK))
    ukv_scale = jnp.concatenate([jnp.ones((MLA_NOPE_DIM,), F32), jnp.full((MLA_V_DIM,), DN_BETA, F32)])
    ev_w_ukv = (nrm((N_EVEN, MLA_KV_RANK, MLA_HEADS, MLA_NOPE_DIM + MLA_V_DIM), MLA_KV_RANK ** -0.5)
                * ukv_scale).reshape(N_EVEN, MLA_KV_RANK, MLA_HEADS * (MLA_NOPE_DIM + MLA_V_DIM))
    ev_w_o = nrm((N_EVEN, MIX_WIDTH, D_MODEL), MIX_WIDTH ** -0.5 * DN_BETA)
    ev_ln1_g = gain((N_EVEN, D_MODEL))
    ev_ln1_b = nrm((N_EVEN, D_MODEL), 0.02)
    ev_ffn_w_gate = nrm((N_EVEN, D_MODEL, D_FF), D_MODEL ** -0.5)
    ev_ffn_w_up = nrm((N_EVEN, D_MODEL, D_FF), D_MODEL ** -0.5)
    ev_ffn_w_down = nrm((N_EVEN, D_FF, D_MODEL), D_FF ** -0.5 * DN_BETA)
    ev_ln2_g = gain((N_EVEN, D_MODEL))
    ev_ln2_b = nrm((N_EVEN, D_MODEL), 0.02)
    od_w_pw1 = nrm((N_ODD, D_MODEL, 2 * CONV_CH), D_MODEL ** -0.5)
    od_b_pw1 = nrm((N_ODD, 2 * CONV_CH), 0.02)
    od_w_dw = nrm((N_ODD, CONV_WIDTH, CONV_CH), CONV_WIDTH ** -0.5)
    od_b_dw = nrm((N_ODD, CONV_CH), 0.02)
    od_g_cn = gain((N_ODD, CONV_CH))
    od_b_cn = nrm((N_ODD, CONV_CH), 0.02)
    od_w_pw2 = nrm((N_ODD, CONV_CH, D_MODEL), CONV_CH ** -0.5 * DN_BETA)
    od_b_pw2 = nrm((N_ODD, D_MODEL), 0.02)
    od_ln1_g = gain((N_ODD, D_MODEL))
    od_ln1_b = nrm((N_ODD, D_MODEL), 0.02)
    od_w_router = nrm((N_ODD, D_MODEL, N_EXPERTS), D_MODEL ** -0.5)
    od_b_router = nrm((N_ODD, N_EXPERTS), 0.01)
    od_moe_w_gate = nrm((N_ODD, N_EXPERTS, D_MODEL, D_FF), D_MODEL ** -0.5)
    od_moe_w_up = nrm((N_ODD, N_EXPERTS, D_MODEL, D_FF), D_MODEL ** -0.5)
    od_moe_w_down = nrm((N_ODD, N_EXPERTS, D_FF, D_MODEL), D_FF ** -0.5 * DN_BETA)
    od_ln2_g = gain((N_ODD, D_MODEL))
    od_ln2_b = nrm((N_ODD, D_MODEL), 0.02)
    return {"x": x, "meta_tokens": meta_tokens,
            "ev_w_in": ev_w_in, "ev_sinks": ev_sinks, "ev_g_cq": ev_g_cq, "ev_w_uq": ev_w_uq,
            "ev_g_ckv": ev_g_ckv, "ev_w_ukv": ev_w_ukv, "ev_w_o": ev_w_o,
            "ev_ln1_g": ev_ln1_g, "ev_ln1_b": ev_ln1_b,
            "ev_ffn_w_gate": ev_ffn_w_gate, "ev_ffn_w_up": ev_ffn_w_up, "ev_ffn_w_down": ev_ffn_w_down,
            "ev_ln2_g": ev_ln2_g, "ev_ln2_b": ev_ln2_b,
            "od_w_pw1": od_w_pw1, "od_b_pw1": od_b_pw1, "od_w_dw": od_w_dw, "od_b_dw": od_b_dw,
            "od_g_cn": od_g_cn, "od_b_cn": od_b_cn, "od_w_pw2": od_w_pw2, "od_b_pw2": od_b_pw2,
            "od_ln1_g": od_ln1_g, "od_ln1_b": od_ln1_b,
            "od_w_router": od_w_router, "od_b_router": od_b_router,
            "od_moe_w_gate": od_moe_w_gate, "od_moe_w_up": od_moe_w_up, "od_moe_w_down": od_moe_w_down,
            "od_ln2_g": od_ln2_g, "od_ln2_b": od_ln2_b}


def reference(x, meta_tokens,
              ev_w_in, ev_sinks, ev_g_cq, ev_w_uq, ev_g_ckv, ev_w_ukv, ev_w_o,
              ev_ln1_g, ev_ln1_b, ev_ffn_w_gate, ev_ffn_w_up, ev_ffn_w_down, ev_ln2_g, ev_ln2_b,
              od_w_pw1, od_b_pw1, od_w_dw, od_b_dw, od_g_cn, od_b_cn, od_w_pw2, od_b_pw2,
              od_ln1_g, od_ln1_b, od_w_router, od_b_router,
              od_moe_w_gate, od_moe_w_up, od_moe_w_down, od_ln2_g, od_ln2_b):
    B = x.shape[0]
    meta = jnp.broadcast_to(meta_tokens[None].astype(x.dtype), (B, N_META, D_MODEL))
    h = jnp.concatenate([meta, x], axis=1)
    for i in range(DEPTH):
        j = i // 2
        if i % 2 == 0:
            mix = parallel_attention_mixer(h, ev_w_in[j], ev_sinks[j], ev_g_cq[j], ev_w_uq[j],
                                           ev_g_ckv[j], ev_w_ukv[j], ev_w_o[j])
            h = layer_norm(DN_ALPHA * h + mix, ev_ln1_g[j], ev_ln1_b[j])
            ffn = swiglu(h, ev_ffn_w_gate[j], ev_ffn_w_up[j], ev_ffn_w_down[j])
            h = layer_norm(DN_ALPHA * h + ffn, ev_ln2_g[j], ev_ln2_b[j])
        else:
            mix = conformer_conv_module(h, od_w_pw1[j], od_b_pw1[j], od_w_dw[j], od_b_dw[j],
                                        od_g_cn[j], od_b_cn[j], od_w_pw2[j], od_b_pw2[j])
            h = layer_norm(DN_ALPHA * h + mix, od_ln1_g[j], od_ln1_b[j])
            ffn = moe_swiglu(h.reshape(-1, D_MODEL), od_w_router[j], od_b_router[j],
                             od_moe_w_gate[j], od_moe_w_up[j], od_moe_w_down[j]).reshape(h.shape)
            h = layer_norm(DN_ALPHA * h + ffn, od_ln2_g[j], od_ln2_b[j])
    return h[:, N_META:]
```

```python
import functools

import jax
import jax.numpy as jnp
from jax import lax
from jax.experimental import pallas as pl
from jax.experimental.pallas import tpu as pltpu

F32 = jnp.float32
BF16 = jnp.bfloat16

D_MODEL = 2048
DEPTH = 2
N_META = 16
BLOCK = 128
SWA_HEADS = 16
SWA_KV_HEADS = 2
SWA_HEAD_DIM = 64
SWA_GROUP = SWA_HEADS // SWA_KV_HEADS
MLA_HEADS = 8
MLA_Q_RANK = 768
MLA_KV_RANK = 512
MLA_NOPE_DIM = 128
MLA_ROPE_DIM = 64
MLA_V_DIM = 128
MLA_QK_DIM = MLA_NOPE_DIM + MLA_ROPE_DIM
ROPE_THETA = 10000.0
SWA_Q_W = SWA_HEADS * SWA_HEAD_DIM
SWA_KV_W = SWA_KV_HEADS * SWA_HEAD_DIM
CONV_WIDTH = 31
D_FF = 7 * D_MODEL // 2
N_EXPERTS = 8
TOP_K = 2
DN_ALPHA = (2 * DEPTH) ** 0.25
NEG_INF = -1e30
LN_EPS = 1e-5
RMS_EPS = 1e-6

LANE = 128
VMEM_LIMIT = 56 * 1024 * 1024

ROW_TILE = 512
FRONT = ROW_TILE - N_META
MLA_PAD_W = 256
FF_TILE = 512
MOE_ROWS = 512
GATHER_ROWS = 256
HALO = 32


def _cparams(sem):
    return pltpu.CompilerParams(dimension_semantics=sem, vmem_limit_bytes=VMEM_LIMIT)


def _layer_norm(x, g, b):
    mu = jnp.mean(x, axis=-1, keepdims=True)
    xc = x - mu
    var = jnp.mean(xc * xc, axis=-1, keepdims=True)
    return xc * lax.rsqrt(var + LN_EPS) * g + b


def _rms_norm(x, g):
    return x * lax.rsqrt(jnp.mean(x * x, axis=-1, keepdims=True) + RMS_EPS) * g


def _sigmoid(x):
    return 1.0 / (1.0 + jnp.exp(-x))


def _dot(a, b):
    return jnp.dot(a, b, preferred_element_type=F32)


def _dot_nt(a, b):
    return lax.dot_general(a, b, (((1,), (1,)), ((), ())), preferred_element_type=F32)


IN_ALL_W = SWA_Q_W + 2 * SWA_KV_W + MLA_Q_RANK + MLA_KV_RANK + 2 * LANE


def _inproj_kernel(x_ref, w_ref, cs_ref, aq_ref, akv_ref, cq_ref, ckv_ref, kr_ref):
    p = _dot(x_ref[...].astype(BF16), w_ref[...])
    o = 0
    aq_ref[...] = p[:, o:o + SWA_Q_W].astype(BF16)
    o += SWA_Q_W
    akv_ref[...] = p[:, o:o + 2 * SWA_KV_W].astype(BF16)
    o += 2 * SWA_KV_W
    cq_ref[...] = p[:, o:o + MLA_Q_RANK]
    o += MLA_Q_RANK
    ckv_ref[...] = p[:, o:o + MLA_KV_RANK]
    o += MLA_KV_RANK
    kr = p[:, o:o + LANE]
    kr_rot = p[:, o + LANE:o + 2 * LANE]
    kr_ref[...] = (kr * cs_ref[:, :LANE] + kr_rot * cs_ref[:, LANE:]).astype(BF16)


def _inproj(h, w_all, cs, tm=256):
    tp = h.shape[0]
    row = lambda w: pl.BlockSpec((tm, w), lambda i: (i, 0))
    return pl.pallas_call(
        _inproj_kernel,
        grid=(tp // tm,),
        in_specs=[row(D_MODEL),
                  pl.BlockSpec((D_MODEL, IN_ALL_W), lambda i: (0, 0)),
                  row(2 * LANE)],
        out_specs=[row(SWA_Q_W), row(2 * SWA_KV_W), row(MLA_Q_RANK), row(MLA_KV_RANK), row(LANE)],
        out_shape=[jax.ShapeDtypeStruct((tp, SWA_Q_W), BF16),
                   jax.ShapeDtypeStruct((tp, 2 * SWA_KV_W), BF16),
                   jax.ShapeDtypeStruct((tp, MLA_Q_RANK), F32),
                   jax.ShapeDtypeStruct((tp, MLA_KV_RANK), F32),
                   jax.ShapeDtypeStruct((tp, LANE), BF16)],
        compiler_params=_cparams(("parallel",)),
        name="inproj",
    )(h, w_all, cs)


MLA_Q_ALL_W = MLA_HEADS * MLA_PAD_W + MLA_HEADS * LANE


def _mla_up_kernel(cq_ref, ckv_ref, kr_ref, cs_ref, gq_ref, gkv_ref, wq_ref, wkv_ref,
                   q_ref, kk_ref, vv_ref):
    scale = MLA_QK_DIM ** -0.5
    cos = cs_ref[:, :LANE]
    sin = cs_ref[:, LANE:]
    qa = _dot(_rms_norm(cq_ref[...], gq_ref[...]).astype(BF16), wq_ref[...])
    rot0 = MLA_HEADS * MLA_PAD_W
    for h in range(MLA_HEADS):
        a = h * MLA_PAD_W
        q_ref[:, a:a + LANE] = (qa[:, a:a + LANE] * scale).astype(BF16)
        roped = qa[:, a + LANE:a + 2 * LANE] * cos + qa[:, rot0 + h * LANE:rot0 + (h + 1) * LANE] * sin
        q_ref[:, a + LANE:a + 2 * LANE] = (roped * scale).astype(BF16)
    kv = _dot(_rms_norm(ckv_ref[...], gkv_ref[...]).astype(BF16), wkv_ref[...])
    kr = kr_ref[...]
    for h in range(MLA_HEADS):
        a = h * (MLA_NOPE_DIM + MLA_V_DIM)
        kk_ref[:, h * MLA_PAD_W:h * MLA_PAD_W + LANE] = kv[:, a:a + MLA_NOPE_DIM].astype(BF16)
        kk_ref[:, h * MLA_PAD_W + LANE:(h + 1) * MLA_PAD_W] = kr
        vv_ref[:, h * MLA_V_DIM:(h + 1) * MLA_V_DIM] = kv[:, a + MLA_NOPE_DIM:a + MLA_NOPE_DIM + MLA_V_DIM].astype(BF16)


def _mla_up(cq, ckv, kr, cs, gq, gkv, wq_all, wkv, tm=256):
    tp = cq.shape[0]
    row = lambda w: pl.BlockSpec((tm, w), lambda i: (i, 0))
    full = lambda a, b: pl.BlockSpec((a, b), lambda i: (0, 0))
    kw = MLA_HEADS * MLA_PAD_W
    vw = MLA_HEADS * MLA_V_DIM
    return pl.pallas_call(
        _mla_up_kernel,
        grid=(tp // tm,),
        in_specs=[row(MLA_Q_RANK), row(MLA_KV_RANK), row(LANE), row(2 * LANE),
                  full(1, MLA_Q_RANK), full(1, MLA_KV_RANK),
                  full(MLA_Q_RANK, MLA_Q_ALL_W), full(MLA_KV_RANK, MLA_HEADS * (MLA_NOPE_DIM + MLA_V_DIM))],
        out_specs=[row(kw), row(kw), row(vw)],
        out_shape=[jax.ShapeDtypeStruct((tp, kw), BF16),
                   jax.ShapeDtypeStruct((tp, kw), BF16),
                   jax.ShapeDtypeStruct((tp, vw), BF16)],
        compiler_params=_cparams(("parallel",)),
        name="mla_up",
    )(cq, ckv, kr, cs, gq, gkv, wq_all, wkv)


META_BLOCK = FRONT // BLOCK
META_ROW0 = FRONT % BLOCK


def _swa_kernel(sink_ref, q_ref, kvc_ref, kvp_ref, kvm_ref, o_ref):
    n = pl.program_id(1) - META_BLOCK
    q = q_ref[0]
    kvc = kvc_ref[0]
    kvp = kvp_ref[0]
    kvm = kvm_ref[0][META_ROW0:, :]
    qi = lax.broadcasted_iota(jnp.int32, (BLOCK, 2 * BLOCK), 0)
    kj = lax.broadcasted_iota(jnp.int32, (BLOCK, 2 * BLOCK), 1)
    band_ok = (kj > qi) & (kj <= qi + BLOCK) & (kj >= 2 * BLOCK - n * BLOCK)
    mi = lax.broadcasted_iota(jnp.int32, (BLOCK, N_META), 0)
    mm = lax.broadcasted_iota(jnp.int32, (BLOCK, N_META), 1)
    meta_ok = mm <= n * BLOCK + mi - META_ROW0
    outs = []
    for g in range(SWA_KV_HEADS):
        ks = slice(g * SWA_HEAD_DIM, (g + 1) * SWA_HEAD_DIM)
        vs = slice(SWA_KV_W + g * SWA_HEAD_DIM, SWA_KV_W + (g + 1) * SWA_HEAD_DIM)
        k_band = jnp.concatenate([kvp[:, ks], kvc[:, ks]], axis=0)
        v_band = jnp.concatenate([kvp[:, vs], kvc[:, vs]], axis=0)
        k_meta = kvm[:, ks]
        v_meta = kvm[:, vs]
        for hh in range(SWA_GROUP):
            h = g * SWA_GROUP + hh
            qh = q[:, h * SWA_HEAD_DIM:(h + 1) * SWA_HEAD_DIM]
            sb = jnp.where(band_ok, _dot_nt(qh, k_band), NEG_INF)
            sm = jnp.where(meta_ok, _dot_nt(qh, k_meta), NEG_INF)
            sink = sink_ref[h]
            m = jnp.maximum(jnp.maximum(jnp.max(sb, axis=-1, keepdims=True),
                                        jnp.max(sm, axis=-1, keepdims=True)), sink)
            pb = jnp.exp(sb - m)
            pm = jnp.exp(sm - m)
            den = (jnp.sum(pb, axis=-1, keepdims=True) + jnp.sum(pm, axis=-1, keepdims=True)
                   + jnp.exp(sink - m))
            o = _dot(pb.astype(BF16), v_band) + _dot(pm.astype(BF16), v_meta)
            outs.append(o / den)
    o_ref[0] = jnp.concatenate(outs, axis=1).astype(BF16)


def _swa(aq, akv, sinks):
    b, lp, _ = aq.shape
    nb = lp // BLOCK
    kvspec = lambda f: pl.BlockSpec((1, BLOCK, 2 * SWA_KV_W), f)
    return pl.pallas_call(
        _swa_kernel,
        grid=(b, nb),
        in_specs=[pl.BlockSpec(memory_space=pltpu.SMEM),
                  pl.BlockSpec((1, BLOCK, SWA_Q_W), lambda bi, n: (bi, n, 0)),
                  kvspec(lambda bi, n: (bi, n, 0)),
                  kvspec(lambda bi, n: (bi, jnp.maximum(n - 1, 0), 0)),
                  kvspec(lambda bi, n: (bi, META_BLOCK, 0))],
        out_specs=pl.BlockSpec((1, BLOCK, SWA_Q_W), lambda bi, n: (bi, n, 0)),
        out_shape=jax.ShapeDtypeStruct((b, lp, SWA_Q_W), BF16),
        compiler_params=_cparams(("parallel", "parallel")),
        name="swa",
    )(sinks, aq, akv, akv, akv)


def _mla_kernel(q_ref, k_ref, v_ref, o_ref, *, tq):
    qi = pl.program_id(2)
    q = q_ref[0]

    def step(ki, carry, masked):
        m, l, acc = carry
        start = pl.multiple_of(ki * tq, tq)
        kb = k_ref[0, pl.ds(start, tq), :]
        vb = v_ref[0, pl.ds(start, tq), :]
        s = _dot_nt(q, kb)
        if masked:
            qidx = qi * tq + lax.broadcasted_iota(jnp.int32, (tq, tq), 0)
            kidx = ki * tq + lax.broadcasted_iota(jnp.int32, (tq, tq), 1)
            s = jnp.where((kidx <= qidx) & (kidx >= FRONT), s, NEG_INF)
        m_new = jnp.maximum(m, jnp.max(s, axis=-1, keepdims=True))
        alpha = jnp.exp(m - m_new)
        p = jnp.exp(s - m_new)
        l = alpha * l + jnp.sum(p, axis=-1, keepdims=True)
        acc = alpha * acc + _dot(p.astype(BF16), vb)
        return m_new, l, acc

    init = (jnp.full((tq, 1), NEG_INF, F32), jnp.zeros((tq, 1), F32), jnp.zeros((tq, MLA_V_DIM), F32))
    carry = step(0, init, True)
    carry = lax.fori_loop(1, qi, lambda ki, c: step(ki, c, False), carry)
    carry = lax.cond(qi > 0, lambda c: step(qi, c, True), lambda c: c, carry)
    _, l, acc = carry
    o_ref[0] = (acc / l).astype(BF16)


def _mla(q, kk, vv, tq=ROW_TILE):
    b, lp, _ = q.shape
    return pl.pallas_call(
        functools.partial(_mla_kernel, tq=tq),
        grid=(b, MLA_HEADS, lp // tq),
        in_specs=[pl.BlockSpec((1, tq, MLA_PAD_W), lambda bi, h, i: (bi, i, h)),
                  pl.BlockSpec((1, lp, MLA_PAD_W), lambda bi, h, i: (bi, 0, h)),
                  pl.BlockSpec((1, lp, MLA_V_DIM), lambda bi, h, i: (bi, 0, h))],
        out_specs=pl.BlockSpec((1, tq, MLA_V_DIM), lambda bi, h, i: (bi, i, h)),
        out_shape=jax.ShapeDtypeStruct((b, lp, MLA_HEADS * MLA_V_DIM), BF16),
        compiler_params=_cparams(("parallel", "parallel", "arbitrary")),
        name="mla",
    )(q, kk, vv)


def _outproj_kernel(oa_ref, ob_ref, h_ref, w_ref, g_ref, b_ref, o_ref):
    mix = _dot(oa_ref[...], w_ref[:SWA_Q_W, :]) + _dot(ob_ref[...], w_ref[SWA_Q_W:, :])
    o_ref[...] = _layer_norm(DN_ALPHA * h_ref[...] + mix, g_ref[...], b_ref[...])


def _outproj(oa, ob, h, w_o, g, b, tm=ROW_TILE):
    tp = h.shape[0]
    row = lambda w: pl.BlockSpec((tm, w), lambda i: (i, 0))
    full = lambda a, c: pl.BlockSpec((a, c), lambda i: (0, 0))
    return pl.pallas_call(
        _outproj_kernel,
        grid=(tp // tm,),
        in_specs=[row(SWA_Q_W), row(MLA_HEADS * MLA_V_DIM), row(D_MODEL),
                  full(D_MODEL, D_MODEL), full(1, D_MODEL), full(1, D_MODEL)],
        out_specs=row(D_MODEL),
        out_shape=jax.ShapeDtypeStruct((tp, D_MODEL), F32),
        compiler_params=_cparams(("parallel",)),
        name="outproj_ln",
    )(oa, ob, h, w_o, g, b)


def _ffn_dense_kernel(be_ref, x_ref, wg_ref, wu_ref, wd_ref, g_ref, b_ref, o_ref, xb_scr, acc_scr):
    j = pl.program_id(1)

    @pl.when(j == 0)
    def _():
        xb_scr[...] = x_ref[...].astype(BF16)
        acc_scr[...] = jnp.zeros_like(acc_scr)

    xb = xb_scr[...]
    a = _dot(xb, wg_ref[0])
    u = _dot(xb, wu_ref[0])
    act = (a * _sigmoid(a) * u).astype(BF16)
    acc_scr[...] += _dot(act, wd_ref[0])

    @pl.when(j == pl.num_programs(1) - 1)
    def _():
        o_ref[...] = _layer_norm(DN_ALPHA * x_ref[...] + acc_scr[...], g_ref[...], b_ref[...])


def _ffn_routed_kernel(be_ref, x_ref, wg_ref, wu_ref, wd_ref, o_ref, acc_scr):
    j = pl.program_id(1)

    @pl.when(j == 0)
    def _():
        acc_scr[...] = jnp.zeros_like(acc_scr)

    xb = x_ref[...]
    a = _dot(xb, wg_ref[0])
    u = _dot(xb, wu_ref[0])
    act = (a * _sigmoid(a) * u).astype(BF16)
    acc_scr[...] += _dot(act, wd_ref[0])

    @pl.when(j == pl.num_programs(1) - 1)
    def _():
        o_ref[...] = acc_scr[...]


def _ffn(x, block_expert, w_gate, w_up, w_down, ln=None, tm=ROW_TILE, tf=FF_TILE):
    rows = x.shape[0]
    row = pl.BlockSpec((tm, D_MODEL), lambda i, j, be: (i, 0))
    in_specs = [row,
                pl.BlockSpec((1, D_MODEL, tf), lambda i, j, be: (be[i], 0, j)),
                pl.BlockSpec((1, D_MODEL, tf), lambda i, j, be: (be[i], 0, j)),
                pl.BlockSpec((1, tf, D_MODEL), lambda i, j, be: (be[i], j, 0))]
    scratch = [pltpu.VMEM((tm, D_MODEL), F32)]
    args = [block_expert, x, w_gate, w_up, w_down]
    if ln is not None:
        vec = pl.BlockSpec((1, D_MODEL), lambda i, j, be: (0, 0))
        in_specs += [vec, vec]
        scratch = [pltpu.VMEM((tm, D_MODEL), BF16)] + scratch
        args += list(ln)
        body, name = _ffn_dense_kernel, "ffn_dense_ln"
    else:
        body, name = _ffn_routed_kernel, "ffn_routed"
    return pl.pallas_call(
        body,
        grid_spec=pltpu.PrefetchScalarGridSpec(
            num_scalar_prefetch=1,
            grid=(rows // tm, D_FF // tf),
            in_specs=in_specs,
            out_specs=row,
            scratch_shapes=scratch),
        out_shape=jax.ShapeDtypeStruct((rows, D_MODEL), F32),
        compiler_params=_cparams(("parallel", "arbitrary")),
        name=name,
    )(*args)


def _pw1_kernel(h_ref, wa_ref, wg_ref, ba_ref, bg_ref, u_ref, xb_scr):
    @pl.when(pl.program_id(1) == 0)
    def _():
        xb_scr[...] = h_ref[...].astype(BF16)

    xb = xb_scr[...]
    a = _dot(xb, wa_ref[...]) + ba_ref[...]
    g = _dot(xb, wg_ref[...]) + bg_ref[...]
    u_ref[...] = a * _sigmoid(g)


def _pw1(h, w_pw1, b_pw1, tm=ROW_TILE, tn=512):
    tp = h.shape[0]
    nj = D_MODEL // tn
    return pl.pallas_call(
        _pw1_kernel,
        grid=(tp // tm, nj),
        in_specs=[pl.BlockSpec((tm, D_MODEL), lambda i, j: (i, 0)),
                  pl.BlockSpec((D_MODEL, tn), lambda i, j: (0, j)),
                  pl.BlockSpec((D_MODEL, tn), lambda i, j: (0, j + nj)),
                  pl.BlockSpec((1, tn), lambda i, j: (0, j)),
                  pl.BlockSpec((1, tn), lambda i, j: (0, j + nj))],
        out_specs=pl.BlockSpec((tm, tn), lambda i, j: (i, j)),
        out_shape=jax.ShapeDtypeStruct((tp, D_MODEL), F32),
        scratch_shapes=[pltpu.VMEM((tm, D_MODEL), BF16)],
        compiler_params=_cparams(("parallel", "arbitrary")),
        name="pw1_glu",
    )(h, w_pw1, w_pw1, b_pw1, b_pw1)


CONV_ROWS = 32
CONV_COLS = 256


def _conv_kernel(uc_ref, up_ref, h_ref, wdw_ref, bdw_ref, gcn_ref, bcn_ref, w2_ref, b2_ref,
                 g1_ref, b1_ref, wr_ref, br_ref, ho_ref, idx_ref, gate_ref, win_scr, cv_scr,
                 *, tm, tiles_per_batch):
    row0 = (pl.program_id(0) % tiles_per_batch) * tm
    win_scr[HALO:, :] = uc_ref[...]
    win_scr[:HALO, :] = up_ref[...]

    @pl.when(row0 < FRONT)
    def _():
        r = lax.broadcasted_iota(jnp.int32, (tm + HALO, 1), 0) + (row0 - HALO)
        win_scr[...] = jnp.where(r >= FRONT, win_scr[...], 0.0)

    off = HALO - (CONV_WIDTH - 1)

    def rows_body(rc, _):
        r0 = pl.multiple_of(rc * CONV_ROWS, CONV_ROWS)
        for cc in range(D_MODEL // CONV_COLS):
            cs = slice(cc * CONV_COLS, (cc + 1) * CONV_COLS)
            blk = win_scr[pl.ds(r0, CONV_ROWS + HALO), cs]
            acc = jnp.zeros((CONV_ROWS, CONV_COLS), F32)
            for k in range(CONV_WIDTH):
                acc = acc + blk[off + k:off + k + CONV_ROWS, :] * wdw_ref[k:k + 1, cs]
            cv_scr[pl.ds(r0, CONV_ROWS), cs] = acc
        return 0

    lax.fori_loop(0, tm // CONV_ROWS, rows_body, 0)

    c = _layer_norm(cv_scr[...] + bdw_ref[...], gcn_ref[...], bcn_ref[...])
    c = c * _sigmoid(c)
    mix = _dot(c.astype(BF16), w2_ref[...]) + b2_ref[...]
    hn = _layer_norm(DN_ALPHA * h_ref[...] + mix, g1_ref[...], b1_ref[...])
    ho_ref[...] = hn

    logits = jnp.dot(hn, wr_ref[...], preferred_element_type=F32,
                     precision=lax.Precision.HIGHEST) + br_ref[...]
    lane = lax.broadcasted_iota(jnp.int32, logits.shape, 1)
    lg = jnp.where(lane < N_EXPERTS, logits, -jnp.inf)
    m1 = jnp.max(lg, axis=-1, keepdims=True)
    i1 = jnp.min(jnp.where(lg == m1, lane, LANE), axis=-1, keepdims=True)
    lg2 = jnp.where(lane == i1, -jnp.inf, lg)
    m2 = jnp.max(lg2, axis=-1, keepdims=True)
    i2 = jnp.min(jnp.where(lg2 == m2, lane, LANE), axis=-1, keepdims=True)
    e = jnp.exp(m2 - m1)
    gate1 = 1.0 / (1.0 + e)
    gate2 = e * gate1
    idx_ref[...] = jnp.where(lane == 0, i1, jnp.where(lane == 1, i2, 0))
    gate_ref[...] = jnp.where(lane == 0, gate1, jnp.where(lane == 1, gate2, 0.0))


def _conv_module(u, h, wdw, bdw, gcn, bcn, w2, b2, g1, b1, wr, br, tiles_per_batch, tm=256):
    tp = h.shape[0]
    row = lambda w: pl.BlockSpec((tm, w), lambda i: (i, 0))
    full = lambda a, c: pl.BlockSpec((a, c), lambda i: (0, 0))
    halo_blocks = tm // HALO
    return pl.pallas_call(
        functools.partial(_conv_kernel, tm=tm, tiles_per_batch=tiles_per_batch * (ROW_TILE // tm)),
        grid=(tp // tm,),
        in_specs=[row(D_MODEL),
                  pl.BlockSpec((HALO, D_MODEL), lambda i: (jnp.maximum(i * halo_blocks - 1, 0), 0)),
                  row(D_MODEL),
                  full(HALO, D_MODEL), full(1, D_MODEL), full(1, D_MODEL), full(1, D_MODEL),
                  full(D_MODEL, D_MODEL), full(1, D_MODEL), full(1, D_MODEL), full(1, D_MODEL),
                  full(D_MODEL, LANE), full(1, LANE)],
        out_specs=[row(D_MODEL), row(LANE), row(LANE)],
        out_shape=[jax.ShapeDtypeStruct((tp, D_MODEL), F32),
                   jax.ShapeDtypeStruct((tp, LANE), jnp.int32),
                   jax.ShapeDtypeStruct((tp, LANE), F32)],
        scratch_shapes=[pltpu.VMEM((tm + HALO, D_MODEL), F32), pltpu.VMEM((tm, D_MODEL), F32)],
        compiler_params=_cparams(("parallel",)),
        name="conv_module_ln_router",
    )(u, u, h, wdw, bdw, gcn, bcn, w2, b2, g1, b1, wr, br)


def _row_copy(src_hbm, row, buf, slot, sem):
    return pltpu.make_async_copy(src_hbm.at[pl.ds(row, 1), :], buf.at[pl.ds(slot, 1), :], sem)


def _gather_kernel(idx_ref, src_hbm, o_ref, buf, sem, *, rows):
    def issue(r, _):
        _row_copy(src_hbm, idx_ref[0, 0, r], buf, r, sem).start()
        return 0

    lax.fori_loop(0, rows, issue, 0)

    def drain(r, _):
        _row_copy(src_hbm, 0, buf, r, sem).wait()
        return 0

    lax.fori_loop(0, rows, drain, 0)
    o_ref[...] = buf[...].astype(o_ref.dtype)


def _gather_rows(src, idx, rows=GATHER_ROWS):
    n = idx.shape[0]
    steps = n // rows
    return pl.pallas_call(
        functools.partial(_gather_kernel, rows=rows),
        grid=(steps,),
        in_specs=[pl.BlockSpec((1, 1, rows), lambda i: (i, 0, 0), memory_space=pltpu.SMEM),
                  pl.BlockSpec(memory_space=pl.ANY)],
        out_specs=pl.BlockSpec((rows, D_MODEL), lambda i: (i, 0)),
        out_shape=jax.ShapeDtypeStruct((n, D_MODEL), BF16),
        scratch_shapes=[pltpu.VMEM((rows, D_MODEL), F32), pltpu.SemaphoreType.DMA(())],
        compiler_params=_cparams(("arbitrary",)),
        name="dispatch_gather",
    )(idx.reshape(steps, 1, rows), src)


def _combine_kernel(i0_ref, i1_ref, y_hbm, h_ref, gate_ref, g_ref, b_ref, o_ref, buf0, buf1, sem, *, rows):
    def issue(r, _):
        _row_copy(y_hbm, i0_ref[0, 0, r], buf0, r, sem).start()
        _row_copy(y_hbm, i1_ref[0, 0, r], buf1, r, sem).start()
        return 0

    lax.fori_loop(0, rows, issue, 0)

    def drain(r, _):
        _row_copy(y_hbm, 0, buf0, r, sem).wait()
        _row_copy(y_hbm, 0, buf1, r, sem).wait()
        return 0

    lax.fori_loop(0, rows, drain, 0)
    gate = gate_ref[...]
    y = buf0[...] * gate[:, 0:1] + buf1[...] * gate[:, 1:2]
    o_ref[...] = _layer_norm(DN_ALPHA * h_ref[...] + y, g_ref[...], b_ref[...])


def _combine(y, slot0, slot1, h, gate, g, b, batch, seq, lp, rows=GATHER_ROWS):
    per_batch = seq // rows
    lead = ROW_TILE // rows
    lp_blocks = lp // rows
    hrow = lambda w: pl.BlockSpec((rows, w), lambda bi, j: (bi * lp_blocks + lead + j, 0))
    islot = pl.BlockSpec((1, 1, rows), lambda bi, j: (bi * per_batch + j, 0, 0), memory_space=pltpu.SMEM)
    vec = pl.BlockSpec((1, D_MODEL), lambda bi, j: (0, 0))
    steps = batch * per_batch
    return pl.pallas_call(
        functools.partial(_combine_kernel, rows=rows),
        grid=(batch, per_batch),
        in_specs=[islot, islot, pl.BlockSpec(memory_space=pl.ANY), hrow(D_MODEL), hrow(LANE), vec, vec],
        out_specs=pl.BlockSpec((rows, D_MODEL), lambda bi, j: (bi * per_batch + j, 0)),
        out_shape=jax.ShapeDtypeStruct((batch * seq, D_MODEL), F32),
        scratch_shapes=[pltpu.VMEM((rows, D_MODEL), F32), pltpu.VMEM((rows, D_MODEL), F32),
                        pltpu.SemaphoreType.DMA(())],
        compiler_params=_cparams(("arbitrary", "arbitrary")),
        name="combine_ln",
    )(slot0.reshape(steps, 1, rows), slot1.reshape(steps, 1, rows), y, h, gate, g, b)


def _rot_cols(w):
    half = MLA_ROPE_DIM // 2
    return jnp.concatenate([-w[..., half:], w[..., :half]], axis=-1)


def _rope_table(batch, lp):
    half = MLA_ROPE_DIM // 2
    inv = ROPE_THETA ** (-jnp.arange(half, dtype=F32) * 2.0 / MLA_ROPE_DIM)
    pos = (jnp.arange(lp) - FRONT).astype(F32)
    ang = pos[:, None] * inv[None, :]
    z = jnp.zeros((lp, LANE - MLA_ROPE_DIM), F32)
    cos = jnp.concatenate([jnp.cos(ang), jnp.cos(ang), z], axis=-1)
    sin = jnp.concatenate([jnp.sin(ang), jnp.sin(ang), z], axis=-1)
    return jnp.tile(jnp.concatenate([cos, sin], axis=-1), (batch, 1))


def _pad_cols(w, width):
    return jnp.pad(w, ((0, 0), (0, width - w.shape[-1])))


def _routing(top_idx, batch, seq, lp):
    t = batch * seq
    a = t * TOP_K
    real = top_idx.reshape(batch, lp, LANE)[:, ROW_TILE:, :TOP_K]
    flat_e = real.reshape(a)
    tok_row = (jnp.arange(batch)[:, None] * lp + ROW_TILE + jnp.arange(seq)[None, :]).reshape(t)
    flat_row = jnp.repeat(tok_row, TOP_K).astype(jnp.int32)
    onehot = (flat_e[:, None] == jnp.arange(N_EXPERTS)[None, :]).astype(jnp.int32)
    csum = jnp.cumsum(onehot, axis=0)
    rank = jnp.take_along_axis(csum, flat_e[:, None], axis=1)[:, 0] - 1
    counts = csum[-1]
    padded = (counts + MOE_ROWS - 1) // MOE_ROWS * MOE_ROWS
    pend = jnp.cumsum(padded)
    pstart = pend - padded
    dest = (pstart[flat_e] + rank).astype(jnp.int32)
    nblk = a // MOE_ROWS + N_EXPERTS
    p = nblk * MOE_ROWS
    src = jnp.zeros((p,), jnp.int32).at[dest].set(flat_row)
    blk_e = jnp.minimum(jnp.sum(jnp.arange(nblk)[:, None] * MOE_ROWS >= pend[None, :], -1),
                        N_EXPERTS - 1).astype(jnp.int32)
    dest2 = dest.reshape(t, TOP_K)
    return src, blk_e, dest2[:, 0], dest2[:, 1]


def kernel(x, meta_tokens, ev_w_in, ev_sinks, ev_g_cq, ev_w_uq, ev_g_ckv, ev_w_ukv, ev_w_o, ev_ln1_g, ev_ln1_b, ev_ffn_w_gate, ev_ffn_w_up, ev_ffn_w_down, ev_ln2_g, ev_ln2_b, od_w_pw1, od_b_pw1, od_w_dw, od_b_dw, od_g_cn, od_b_cn, od_w_pw2, od_b_pw2, od_ln1_g, od_ln1_b, od_w_router, od_b_router, od_moe_w_gate, od_moe_w_up, od_moe_w_down, od_ln2_g, od_ln2_b):
    batch, seq, _ = x.shape
    assert seq % ROW_TILE == 0
    lp = ROW_TILE + seq
    tp = batch * lp
    vec = lambda v: v.reshape(1, -1).astype(F32)

    meta = jnp.broadcast_to(meta_tokens[None].astype(x.dtype), (batch, N_META, D_MODEL))
    h = jnp.concatenate([jnp.zeros((batch, FRONT, D_MODEL), x.dtype), meta, x], axis=1).reshape(tp, D_MODEL)
    cs = _rope_table(batch, lp)

    w_in = ev_w_in[0]
    o = SWA_Q_W + 2 * SWA_KV_W + MLA_Q_RANK + MLA_KV_RANK
    w_kr = w_in[:, o:]
    w_in_all = jnp.concatenate([w_in[:, :SWA_Q_W] * (SWA_HEAD_DIM ** -0.5), w_in[:, SWA_Q_W:o],
                                _pad_cols(w_kr, LANE), _pad_cols(_rot_cols(w_kr), LANE)], axis=1).astype(BF16)
    aq, akv, cq, ckv, kr = _inproj(h, w_in_all, cs)

    w_uq = ev_w_uq[0].reshape(MLA_Q_RANK, MLA_HEADS, MLA_QK_DIM)
    w_uq_pad = jnp.pad(w_uq, ((0, 0), (0, 0), (0, MLA_PAD_W - MLA_QK_DIM))).reshape(MLA_Q_RANK, -1)
    w_uq_rot = jnp.pad(_rot_cols(w_uq[..., MLA_NOPE_DIM:]),
                       ((0, 0), (0, 0), (0, LANE - MLA_ROPE_DIM))).reshape(MLA_Q_RANK, -1)
    wq_all = jnp.concatenate([w_uq_pad, w_uq_rot], axis=1).astype(BF16)
    q, kk, vv = _mla_up(cq, ckv, kr, cs, vec(ev_g_cq[0]), vec(ev_g_ckv[0]), wq_all, ev_w_ukv[0].astype(BF16))

    out_a = _swa(aq.reshape(batch, lp, -1), akv.reshape(batch, lp, -1), ev_sinks[0].astype(F32))
    out_b = _mla(q.reshape(batch, lp, -1), kk.reshape(batch, lp, -1), vv.reshape(batch, lp, -1))
    h = _outproj(out_a.reshape(tp, -1), out_b.reshape(tp, -1), h, ev_w_o[0].astype(BF16),
                 vec(ev_ln1_g[0]), vec(ev_ln1_b[0]))

    dense_blocks = jnp.zeros((tp // ROW_TILE,), jnp.int32)
    h = _ffn(h, dense_blocks, ev_ffn_w_gate.astype(BF16), ev_ffn_w_up.astype(BF16),
             ev_ffn_w_down.astype(BF16), ln=(vec(ev_ln2_g[0]), vec(ev_ln2_b[0])))

    u = _pw1(h, od_w_pw1[0].astype(BF16), vec(od_b_pw1[0]))
    wdw = jnp.pad(od_w_dw[0].astype(F32), ((0, HALO - CONV_WIDTH), (0, 0)))
    h, top_idx, gates = _conv_module(
        u, h, wdw, vec(od_b_dw[0]), vec(od_g_cn[0]), vec(od_b_cn[0]),
        od_w_pw2[0].astype(BF16), vec(od_b_pw2[0]), vec(od_ln1_g[0]), vec(od_ln1_b[0]),
        _pad_cols(od_w_router[0].astype(F32), LANE), _pad_cols(vec(od_b_router[0]), LANE),
        tiles_per_batch=lp // ROW_TILE)

    src, blk_e, slot0, slot1 = _routing(top_idx, batch, seq, lp)
    xg = _gather_rows(h, src)
    y = _ffn(xg, blk_e, od_moe_w_gate[0].astype(BF16), od_moe_w_up[0].astype(BF16),
             od_moe_w_down[0].astype(BF16), tm=MOE_ROWS)
    out = _combine(y, slot0, slot1, h, gates, vec(od_ln2_g[0]), vec(od_ln2_b[0]), batch, seq, lp)
    return out.reshape(batch, seq, D_MODEL)
```

```python
import functools

import jax
import jax.numpy as jnp
from jax import lax
from jax.experimental import pallas as pl
from jax.experimental.pallas import tpu as pltpu

F32 = jnp.float32
BF16 = jnp.bfloat16

D_MODEL = 2048
DEPTH = 2
N_META = 16
BLOCK = 128
SWA_HEADS = 16
SWA_KV_HEADS = 2
SWA_HEAD_DIM = 64
SWA_GROUP = SWA_HEADS // SWA_KV_HEADS
MLA_HEADS = 8
MLA_Q_RANK = 768
MLA_KV_RANK = 512
MLA_NOPE_DIM = 128
MLA_ROPE_DIM = 64
MLA_V_DIM = 128
MLA_QK_DIM = MLA_NOPE_DIM + MLA_ROPE_DIM
ROPE_THETA = 10000.0
SWA_Q_W = SWA_HEADS * SWA_HEAD_DIM
SWA_KV_W = SWA_KV_HEADS * SWA_HEAD_DIM
CONV_WIDTH = 31
D_FF = 7 * D_MODEL // 2
N_EXPERTS = 8
TOP_K = 2
DN_ALPHA = (2 * DEPTH) ** 0.25
NEG_INF = -1e30
LN_EPS = 1e-5
RMS_EPS = 1e-6

LANE = 128
SUBLANE = 8
VMEM_LIMIT = 56 * 1024 * 1024

ROW_TILE = 512
FRONT = ROW_TILE - N_META
MLA_PAD_W = 256
FF_TILE = 512
MOE_ROWS = 512
GATHER_ROWS = 256
HALO = 32


def _cparams(sem):
    return pltpu.CompilerParams(dimension_semantics=sem, vmem_limit_bytes=VMEM_LIMIT)


def _layer_norm(x, g, b):
    mu = jnp.mean(x, axis=-1, keepdims=True)
    xc = x - mu
    var = jnp.mean(xc * xc, axis=-1, keepdims=True)
    return xc * lax.rsqrt(var + LN_EPS) * g + b


def _rms_norm(x, g):
    return x * lax.rsqrt(jnp.mean(x * x, axis=-1, keepdims=True) + RMS_EPS) * g


def _sigmoid(x):
    return 1.0 / (1.0 + jnp.exp(-x))


def _dot(a, b):
    return jnp.dot(a, b, preferred_element_type=F32)


def _dot_nt(a, b):
    return lax.dot_general(a, b, (((1,), (1,)), ((), ())), preferred_element_type=F32)


IN_ALL_W = SWA_Q_W + 2 * SWA_KV_W + MLA_Q_RANK + MLA_KV_RANK + 2 * LANE


def _inproj_kernel(x_ref, w_ref, cs_ref, aq_ref, akv_ref, cq_ref, ckv_ref, kr_ref):
    p = _dot(x_ref[...].astype(BF16), w_ref[...])
    o = 0
    aq_ref[...] = p[:, o:o + SWA_Q_W].astype(BF16)
    o += SWA_Q_W
    akv_ref[...] = p[:, o:o + 2 * SWA_KV_W].astype(BF16)
    o += 2 * SWA_KV_W
    cq_ref[...] = p[:, o:o + MLA_Q_RANK]
    o += MLA_Q_RANK
    ckv_ref[...] = p[:, o:o + MLA_KV_RANK]
    o += MLA_KV_RANK
    kr = p[:, o:o + LANE]
    kr_rot = p[:, o + LANE:o + 2 * LANE]
    kr_ref[...] = (kr * cs_ref[:, :LANE] + kr_rot * cs_ref[:, LANE:]).astype(BF16)


def _inproj(h, w_all, cs, tm=256):
    tp = h.shape[0]
    row = lambda w: pl.BlockSpec((tm, w), lambda i: (i, 0))
    return pl.pallas_call(
        _inproj_kernel,
        grid=(tp // tm,),
        in_specs=[row(D_MODEL),
                  pl.BlockSpec((D_MODEL, IN_ALL_W), lambda i: (0, 0)),
                  row(2 * LANE)],
        out_specs=[row(SWA_Q_W), row(2 * SWA_KV_W), row(MLA_Q_RANK), row(MLA_KV_RANK), row(LANE)],
        out_shape=[jax.ShapeDtypeStruct((tp, SWA_Q_W), BF16),
                   jax.ShapeDtypeStruct((tp, 2 * SWA_KV_W), BF16),
                   jax.ShapeDtypeStruct((tp, MLA_Q_RANK), F32),
                   jax.ShapeDtypeStruct((tp, MLA_KV_RANK), F32),
                   jax.ShapeDtypeStruct((tp, LANE), BF16)],
        compiler_params=_cparams(("parallel",)),
        name="inproj",
    )(h, w_all, cs)


MLA_QT_ROWS = MLA_HEADS * MLA_PAD_W + MLA_HEADS * MLA_ROPE_DIM
LOG2E = 1.4426950408889634


def _mla_up_kernel(cq_ref, ckv_ref, kr_ref, cst_ref, gq_ref, gkv_ref, wqt_ref, wk_ref, wvt_ref,
                   qt_ref, kk_ref, vt_ref):
    scale = MLA_QK_DIM ** -0.5 * LOG2E
    tm = cq_ref.shape[0]
    cos = cst_ref[:MLA_ROPE_DIM, :]
    sin = cst_ref[MLA_ROPE_DIM:, :]
    cqn = _rms_norm(cq_ref[...], gq_ref[...]).astype(BF16)
    qa = _dot_nt(wqt_ref[...], cqn)
    rot0 = MLA_HEADS * MLA_PAD_W
    for h in range(MLA_HEADS):
        a = h * MLA_PAD_W
        r = a + MLA_NOPE_DIM
        qt_ref[a:r, :] = (qa[a:r, :] * scale).astype(BF16)
        roped = (qa[r:r + MLA_ROPE_DIM, :] * cos
                 + qa[rot0 + h * MLA_ROPE_DIM:rot0 + (h + 1) * MLA_ROPE_DIM, :] * sin)
        qt_ref[r:r + MLA_ROPE_DIM, :] = (roped * scale).astype(BF16)
        qt_ref[r + MLA_ROPE_DIM:a + MLA_PAD_W, :] = jnp.zeros((MLA_PAD_W - MLA_QK_DIM, tm), BF16)
    ckvn = _rms_norm(ckv_ref[...], gkv_ref[...]).astype(BF16)
    kn = _dot(ckvn, wk_ref[...])
    kr = kr_ref[...]
    for h in range(MLA_HEADS):
        kk_ref[:, h * MLA_PAD_W:h * MLA_PAD_W + LANE] = kn[:, h * MLA_NOPE_DIM:(h + 1) * MLA_NOPE_DIM].astype(BF16)
        kk_ref[:, h * MLA_PAD_W + LANE:(h + 1) * MLA_PAD_W] = kr
    vt_ref[0] = _dot_nt(wvt_ref[...], ckvn).astype(BF16)


def _mla_up(cq, ckv, kr, cst, gq, gkv, wqt, wk, wvt, tm=ROW_TILE):
    tp = cq.shape[0]
    row = lambda w: pl.BlockSpec((tm, w), lambda i: (i, 0))
    col = lambda r: pl.BlockSpec((r, tm), lambda i: (0, i))
    full = lambda a, b: pl.BlockSpec((a, b), lambda i: (0, 0))
    kw = MLA_HEADS * MLA_PAD_W
    vw = MLA_HEADS * MLA_V_DIM
    return pl.pallas_call(
        _mla_up_kernel,
        grid=(tp // tm,),
        in_specs=[row(MLA_Q_RANK), row(MLA_KV_RANK), row(LANE), col(2 * MLA_ROPE_DIM),
                  full(1, MLA_Q_RANK), full(1, MLA_KV_RANK),
                  full(MLA_QT_ROWS, MLA_Q_RANK), full(MLA_KV_RANK, MLA_HEADS * MLA_NOPE_DIM),
                  full(vw, MLA_KV_RANK)],
        out_specs=[col(kw), row(kw), pl.BlockSpec((1, vw, tm), lambda i: (i, 0, 0))],
        out_shape=[jax.ShapeDtypeStruct((kw, tp), BF16),
                   jax.ShapeDtypeStruct((tp, kw), BF16),
                   jax.ShapeDtypeStruct((tp // tm, vw, tm), BF16)],
        compiler_params=_cparams(("parallel",)),
        name="mla_up",
    )(cq, ckv, kr, cst, gq, gkv, wqt, wk, wvt)


META_BLOCK = FRONT // BLOCK
META_ROW0 = FRONT % BLOCK


def _swa_kernel(sink_ref, q_ref, kvc_ref, kvp_ref, kvm_ref, o_ref):
    n = pl.program_id(1) - META_BLOCK
    q = q_ref[0]
    kvc = kvc_ref[0]
    kvp = kvp_ref[0]
    kvm = kvm_ref[0][META_ROW0:, :]
    qi = lax.broadcasted_iota(jnp.int32, (BLOCK, 2 * BLOCK), 0)
    kj = lax.broadcasted_iota(jnp.int32, (BLOCK, 2 * BLOCK), 1)
    band_ok = (kj > qi) & (kj <= qi + BLOCK) & (kj >= 2 * BLOCK - n * BLOCK)
    mi = lax.broadcasted_iota(jnp.int32, (BLOCK, N_META), 0)
    mm = lax.broadcasted_iota(jnp.int32, (BLOCK, N_META), 1)
    meta_ok = mm <= n * BLOCK + mi - META_ROW0
    outs = []
    for g in range(SWA_KV_HEADS):
        ks = slice(g * SWA_HEAD_DIM, (g + 1) * SWA_HEAD_DIM)
        vs = slice(SWA_KV_W + g * SWA_HEAD_DIM, SWA_KV_W + (g + 1) * SWA_HEAD_DIM)
        k_band = jnp.concatenate([kvp[:, ks], kvc[:, ks]], axis=0)
        v_band = jnp.concatenate([kvp[:, vs], kvc[:, vs]], axis=0)
        k_meta = kvm[:, ks]
        v_meta = kvm[:, vs]
        for hh in range(SWA_GROUP):
            h = g * SWA_GROUP + hh
            qh = q[:, h * SWA_HEAD_DIM:(h + 1) * SWA_HEAD_DIM]
            sb = jnp.where(band_ok, _dot_nt(qh, k_band), NEG_INF)
            sm = jnp.where(meta_ok, _dot_nt(qh, k_meta), NEG_INF)
            sink = sink_ref[h]
            m = jnp.maximum(jnp.maximum(jnp.max(sb, axis=-1, keepdims=True),
                                        jnp.max(sm, axis=-1, keepdims=True)), sink)
            pb = jnp.exp(sb - m)
            pm = jnp.exp(sm - m)
            den = (jnp.sum(pb, axis=-1, keepdims=True) + jnp.sum(pm, axis=-1, keepdims=True)
                   + jnp.exp(sink - m))
            o = _dot(pb.astype(BF16), v_band) + _dot(pm.astype(BF16), v_meta)
            outs.append(o / den)
    o_ref[0] = jnp.concatenate(outs, axis=1).astype(BF16)


def _swa(aq, akv, sinks):
    b, lp, _ = aq.shape
    nb = lp // BLOCK
    kvspec = lambda f: pl.BlockSpec((1, BLOCK, 2 * SWA_KV_W), f)
    return pl.pallas_call(
        _swa_kernel,
        grid=(b, nb),
        in_specs=[pl.BlockSpec(memory_space=pltpu.SMEM),
                  pl.BlockSpec((1, BLOCK, SWA_Q_W), lambda bi, n: (bi, n, 0)),
                  kvspec(lambda bi, n: (bi, n, 0)),
                  kvspec(lambda bi, n: (bi, jnp.maximum(n - 1, 0), 0)),
                  kvspec(lambda bi, n: (bi, META_BLOCK, 0))],
        out_specs=pl.BlockSpec((1, BLOCK, SWA_Q_W), lambda bi, n: (bi, n, 0)),
        out_shape=jax.ShapeDtypeStruct((b, lp, SWA_Q_W), BF16),
        compiler_params=_cparams(("parallel", "parallel")),
        name="swa",
    )(sinks, aq, akv, akv, akv)


def _mla_kernel(qt_ref, k_ref, vt_ref, o_ref, sa_scr, sb_scr, *, tq):
    qi = pl.program_id(2)
    qt = qt_ref[...]
    meta0 = FRONT // LANE * LANE

    def scores(ki):
        start = pl.multiple_of(ki * tq, tq)
        return _dot(k_ref[0, pl.ds(start, tq), :], qt)

    def masked(s, row0):
        kidx = row0 + lax.broadcasted_iota(jnp.int32, s.shape, 0)
        qidx = qi * tq + lax.broadcasted_iota(jnp.int32, s.shape, 1)
        return jnp.where((kidx <= qidx) & (kidx >= FRONT), s, NEG_INF)

    def update(s, smax, vb, carry):
        m, l, acc = carry
        m_new = jnp.maximum(m, smax)
        alpha = jnp.exp2(m - m_new)
        p = jnp.exp2(s - m_new)
        l = alpha * l + jnp.sum(p, axis=0, keepdims=True)
        acc = alpha * acc + _dot(vb, p.astype(BF16))
        return m_new, l, acc

    def update_masked(s, row0, vb, carry):
        s = masked(s, row0)
        return update(s, jnp.max(s, axis=0, keepdims=True), vb, carry)

    init = (jnp.full((1, tq), NEG_INF, F32), jnp.zeros((1, tq), F32), jnp.zeros((MLA_V_DIM, tq), F32))
    carry = update_masked(_dot(k_ref[0, meta0:tq, :], qt), meta0, vt_ref[0, :, meta0:tq], init)

    def issue(ki, dst):
        s = scores(ki)
        dst[...] = s
        return jnp.max(s, axis=0, keepdims=True)

    def consume(src, smax, ki, carry):
        return update(src[...], smax, vt_ref[ki], carry)

    def diagonal(src, carry):
        return update_masked(src[...], qi * tq, vt_ref[qi], carry)

    def tiles_after_first(carry):
        n_full = qi - 1

        def pair(t, c):
            smax_a, carry = c
            ki = 1 + 2 * t
            smax_b = issue(ki + 1, sb_scr)
            carry = consume(sa_scr, smax_a, ki, carry)
            smax_a = issue(ki + 2, sa_scr)
            carry = consume(sb_scr, smax_b, ki + 1, carry)
            return smax_a, carry

        smax_a, carry = lax.fori_loop(0, n_full // 2, pair, (issue(1, sa_scr), carry))

        def odd_tail(carry):
            issue(qi, sb_scr)
            return diagonal(sb_scr, consume(sa_scr, smax_a, qi - 1, carry))

        return lax.cond(n_full % 2 == 1, odd_tail, lambda c: diagonal(sa_scr, c), carry)

    _, l, acc = lax.cond(qi > 0, tiles_after_first, lambda c: c, carry)
    o_ref[0] = (acc / l).T.astype(BF16)


def _mla(qt, kk, vt, batch, tq=ROW_TILE):
    lp = kk.shape[1]
    nq = lp // tq
    return pl.pallas_call(
        functools.partial(_mla_kernel, tq=tq),
        grid=(batch, MLA_HEADS, nq),
        in_specs=[pl.BlockSpec((MLA_PAD_W, tq), lambda bi, h, i: (h, bi * nq + i)),
                  pl.BlockSpec((1, lp, MLA_PAD_W), lambda bi, h, i: (bi, 0, h)),
                  pl.BlockSpec((nq, MLA_V_DIM, tq), lambda bi, h, i: (bi, h, 0))],
        out_specs=pl.BlockSpec((1, tq, MLA_V_DIM), lambda bi, h, i: (bi, i, h)),
        out_shape=jax.ShapeDtypeStruct((batch, lp, MLA_HEADS * MLA_V_DIM), BF16),
        scratch_shapes=[pltpu.VMEM((tq, tq), F32), pltpu.VMEM((tq, tq), F32)],
        compiler_params=_cparams(("parallel", "parallel", "arbitrary")),
        name="mla",
    )(qt, kk, vt)


def _outproj_kernel(oa_ref, ob_ref, h_ref, w_ref, g_ref, b_ref, o_ref):
    mix = _dot(oa_ref[...], w_ref[:SWA_Q_W, :]) + _dot(ob_ref[...], w_ref[SWA_Q_W:, :])
    o_ref[...] = _layer_norm(DN_ALPHA * h_ref[...] + mix, g_ref[...], b_ref[...])


def _outproj(oa, ob, h, w_o, g, b, tm=ROW_TILE):
    tp = h.shape[0]
    row = lambda w: pl.BlockSpec((tm, w), lambda i: (i, 0))
    full = lambda a, c: pl.BlockSpec((a, c), lambda i: (0, 0))
    return pl.pallas_call(
        _outproj_kernel,
        grid=(tp // tm,),
        in_specs=[row(SWA_Q_W), row(MLA_HEADS * MLA_V_DIM), row(D_MODEL),
                  full(D_MODEL, D_MODEL), full(1, D_MODEL), full(1, D_MODEL)],
        out_specs=row(D_MODEL),
        out_shape=jax.ShapeDtypeStruct((tp, D_MODEL), F32),
        compiler_params=_cparams(("parallel",)),
        name="outproj_ln",
    )(oa, ob, h, w_o, g, b)


def _swiglu_step(xb, wg_ref, wu_ref, wd_ref, acc_scr):
    a = _dot(xb, wg_ref[0])
    u = _dot(xb, wu_ref[0])
    act = (a * _sigmoid(a) * u).astype(BF16)
    acc_scr[...] += _dot(act, wd_ref[0])


def _ffn_dense_kernel(be_ref, nu_ref, x_ref, wg_ref, wu_ref, wd_ref, g_ref, b_ref, o_ref, xb_scr, acc_scr):
    j = pl.program_id(1)

    @pl.when(j == 0)
    def _():
        xb_scr[...] = x_ref[...].astype(BF16)
        acc_scr[...] = jnp.zeros_like(acc_scr)

    _swiglu_step(xb_scr[...], wg_ref, wu_ref, wd_ref, acc_scr)

    @pl.when(j == pl.num_programs(1) - 1)
    def _():
        o_ref[...] = _layer_norm(DN_ALPHA * x_ref[...] + acc_scr[...], g_ref[...], b_ref[...])


def _ffn_routed_kernel(be_ref, nu_ref, x_ref, wg_ref, wu_ref, wd_ref, o_ref, acc_scr):
    j = pl.program_id(1)

    @pl.when(pl.program_id(0) < nu_ref[0])
    def _():
        @pl.when(j == 0)
        def _():
            acc_scr[...] = jnp.zeros_like(acc_scr)

        _swiglu_step(x_ref[...], wg_ref, wu_ref, wd_ref, acc_scr)

        @pl.when(j == pl.num_programs(1) - 1)
        def _():
            o_ref[...] = acc_scr[...]

    @pl.when((pl.program_id(0) >= nu_ref[0]) & (j == pl.num_programs(1) - 1))
    def _():
        o_ref[...] = jnp.zeros_like(o_ref)


def _ffn(x, block_expert, n_used, w_gate, w_up, w_down, ln=None, tm=ROW_TILE, tf=FF_TILE):
    rows = x.shape[0]
    nj = D_FF // tf
    blk = lambda i, nu: jnp.minimum(i, nu[0] - 1)
    ff = lambda i, j, nu: jnp.where(i < nu[0], j, nj - 1)
    row = pl.BlockSpec((tm, D_MODEL), lambda i, j, be, nu: (blk(i, nu), 0))
    in_specs = [row,
                pl.BlockSpec((1, D_MODEL, tf), lambda i, j, be, nu: (be[blk(i, nu)], 0, ff(i, j, nu))),
                pl.BlockSpec((1, D_MODEL, tf), lambda i, j, be, nu: (be[blk(i, nu)], 0, ff(i, j, nu))),
                pl.BlockSpec((1, tf, D_MODEL), lambda i, j, be, nu: (be[blk(i, nu)], ff(i, j, nu), 0))]
    scratch = [pltpu.VMEM((tm, D_MODEL), F32)]
    args = [block_expert, n_used, x, w_gate, w_up, w_down]
    if ln is not None:
        vec = pl.BlockSpec((1, D_MODEL), lambda i, j, be, nu: (0, 0))
        in_specs += [vec, vec]
        scratch = [pltpu.VMEM((tm, D_MODEL), BF16)] + scratch
        args += list(ln)
        body, name = _ffn_dense_kernel, "ffn_dense_ln"
    else:
        body, name = _ffn_routed_kernel, "ffn_routed"
    return pl.pallas_call(
        body,
        grid_spec=pltpu.PrefetchScalarGridSpec(
            num_scalar_prefetch=2,
            grid=(rows // tm, nj),
            in_specs=in_specs,
            out_specs=pl.BlockSpec((tm, D_MODEL), lambda i, j, be, nu: (i, 0)),
            scratch_shapes=scratch),
        out_shape=jax.ShapeDtypeStruct((rows, D_MODEL), F32),
        compiler_params=_cparams(("arbitrary", "arbitrary")),
        name=name,
    )(*args)


def _pw1_kernel(h_ref, wa_ref, wg_ref, ba_ref, bg_ref, u_ref, xb_scr):
    @pl.when(pl.program_id(1) == 0)
    def _():
        xb_scr[...] = h_ref[...].astype(BF16)

    xb = xb_scr[...]
    a = _dot(xb, wa_ref[...]) + ba_ref[...]
    g = _dot(xb, wg_ref[...]) + bg_ref[...]
    u_ref[...] = a * _sigmoid(g)


def _pw1(h, w_pw1, b_pw1, tm=ROW_TILE, tn=512):
    tp = h.shape[0]
    nj = D_MODEL // tn
    return pl.pallas_call(
        _pw1_kernel,
        grid=(tp // tm, nj),
        in_specs=[pl.BlockSpec((tm, D_MODEL), lambda i, j: (i, 0)),
                  pl.BlockSpec((D_MODEL, tn), lambda i, j: (0, j)),
                  pl.BlockSpec((D_MODEL, tn), lambda i, j: (0, j + nj)),
                  pl.BlockSpec((1, tn), lambda i, j: (0, j)),
                  pl.BlockSpec((1, tn), lambda i, j: (0, j + nj))],
        out_specs=pl.BlockSpec((tm, tn), lambda i, j: (i, j)),
        out_shape=jax.ShapeDtypeStruct((tp, D_MODEL), F32),
        scratch_shapes=[pltpu.VMEM((tm, D_MODEL), BF16)],
        compiler_params=_cparams(("parallel", "arbitrary")),
        name="pw1_glu",
    )(h, w_pw1, w_pw1, b_pw1, b_pw1)


CONV_ROWS = 32
CONV_COLS = 256


def _conv_kernel(uc_ref, up_ref, h_ref, wdw_ref, bdw_ref, gcn_ref, bcn_ref, w2_ref, b2_ref,
                 g1_ref, b1_ref, wr_ref, br_ref, ho_ref, idx_ref, gate_ref, win_scr, cv_scr,
                 *, tm, tiles_per_batch):
    row0 = (pl.program_id(0) % tiles_per_batch) * tm
    win_scr[HALO:, :] = uc_ref[...]
    win_scr[:HALO, :] = up_ref[...]

    @pl.when(row0 < FRONT)
    def _():
        r = lax.broadcasted_iota(jnp.int32, (tm + HALO, 1), 0) + (row0 - HALO)
        win_scr[...] = jnp.where(r >= FRONT, win_scr[...], 0.0)

    off = HALO - (CONV_WIDTH - 1)

    def rows_body(rc, _):
        r0 = pl.multiple_of(rc * CONV_ROWS, CONV_ROWS)
        for cc in range(D_MODEL // CONV_COLS):
            cs = slice(cc * CONV_COLS, (cc + 1) * CONV_COLS)
            blk = win_scr[pl.ds(r0, CONV_ROWS + HALO), cs]
            acc = jnp.zeros((CONV_ROWS, CONV_COLS), F32)
            for b in range(SUBLANE):
                ub = pltpu.roll(blk, CONV_ROWS + HALO - (off + b), axis=0)
                for a in range((CONV_WIDTH - 1 - b) // SUBLANE + 1):
                    k = SUBLANE * a + b
                    acc = acc + ub[SUBLANE * a:SUBLANE * a + CONV_ROWS, :] * wdw_ref[k:k + 1, cs]
            cv_scr[pl.ds(r0, CONV_ROWS), cs] = acc
        return 0

    lax.fori_loop(0, tm // CONV_ROWS, rows_body, 0)

    c = _layer_norm(cv_scr[...] + bdw_ref[...], gcn_ref[...], bcn_ref[...])
    c = c * _sigmoid(c)
    mix = _dot(c.astype(BF16), w2_ref[...]) + b2_ref[...]
    hn = _layer_norm(DN_ALPHA * h_ref[...] + mix, g1_ref[...], b1_ref[...])
    ho_ref[...] = hn

    lg = [jnp.sum(hn * wr_ref[e:e + 1, :], axis=-1, keepdims=True) + br_ref[e] for e in range(N_EXPERTS)]

    def top1(vals):
        best = vals[0]
        for v in vals[1:]:
            best = jnp.maximum(best, v)
        idx = jnp.full(best.shape, N_EXPERTS - 1, jnp.int32)
        for e in range(N_EXPERTS - 2, -1, -1):
            idx = jnp.where(vals[e] == best, e, idx)
        return best, idx

    m1, i1 = top1(lg)
    m2, i2 = top1([jnp.where(i1 == e, -jnp.inf, lg[e]) for e in range(N_EXPERTS)])
    lane = lax.broadcasted_iota(jnp.int32, (hn.shape[0], LANE), 1)
    e = jnp.exp(m2 - m1)
    gate1 = 1.0 / (1.0 + e)
    gate2 = e * gate1
    idx_ref[...] = jnp.where(lane == 0, i1, jnp.where(lane == 1, i2, 0))
    gate_ref[...] = jnp.where(lane == 0, gate1, jnp.where(lane == 1, gate2, 0.0))


def _conv_module(u, h, wdw, bdw, gcn, bcn, w2, b2, g1, b1, wr, br, tiles_per_batch, tm=256):
    tp = h.shape[0]
    row = lambda w: pl.BlockSpec((tm, w), lambda i: (i, 0))
    full = lambda a, c: pl.BlockSpec((a, c), lambda i: (0, 0))
    halo_blocks = tm // HALO
    return pl.pallas_call(
        functools.partial(_conv_kernel, tm=tm, tiles_per_batch=tiles_per_batch * (ROW_TILE // tm)),
        grid=(tp // tm,),
        in_specs=[row(D_MODEL),
                  pl.BlockSpec((HALO, D_MODEL), lambda i: (jnp.maximum(i * halo_blocks - 1, 0), 0)),
                  row(D_MODEL),
                  full(HALO, D_MODEL), full(1, D_MODEL), full(1, D_MODEL), full(1, D_MODEL),
                  full(D_MODEL, D_MODEL), full(1, D_MODEL), full(1, D_MODEL), full(1, D_MODEL),
                  full(N_EXPERTS, D_MODEL), pl.BlockSpec(memory_space=pltpu.SMEM)],
        out_specs=[row(D_MODEL), row(LANE), row(LANE)],
        out_shape=[jax.ShapeDtypeStruct((tp, D_MODEL), F32),
                   jax.ShapeDtypeStruct((tp, LANE), jnp.int32),
                   jax.ShapeDtypeStruct((tp, LANE), F32)],
        scratch_shapes=[pltpu.VMEM((tm + HALO, D_MODEL), F32), pltpu.VMEM((tm, D_MODEL), F32)],
        compiler_params=_cparams(("parallel",)),
        name="conv_module_ln_router",
    )(u, u, h, wdw, bdw, gcn, bcn, w2, b2, g1, b1, wr, br)


def _row_copy(src_hbm, row, buf, slot, sem):
    return pltpu.make_async_copy(src_hbm.at[pl.ds(row, 1), :], buf.at[pl.ds(slot, 1), :], sem)


def _gather_kernel(idx_ref, src_hbm, o_ref, buf, sem, *, rows):
    def issue(r, _):
        _row_copy(src_hbm, idx_ref[0, 0, r], buf, r, sem).start()
        return 0

    lax.fori_loop(0, rows, issue, 0, unroll=8)

    def drain(r, _):
        _row_copy(src_hbm, 0, buf, r, sem).wait()
        return 0

    lax.fori_loop(0, rows, drain, 0)
    o_ref[...] = buf[...].astype(o_ref.dtype)


def _gather_rows(src, idx, rows=GATHER_ROWS):
    n = idx.shape[0]
    steps = n // rows
    return pl.pallas_call(
        functools.partial(_gather_kernel, rows=rows),
        grid=(steps,),
        in_specs=[pl.BlockSpec((1, 1, rows), lambda i: (i, 0, 0), memory_space=pltpu.SMEM),
                  pl.BlockSpec(memory_space=pl.ANY)],
        out_specs=pl.BlockSpec((rows, D_MODEL), lambda i: (i, 0)),
        out_shape=jax.ShapeDtypeStruct((n, D_MODEL), BF16),
        scratch_shapes=[pltpu.VMEM((rows, D_MODEL), F32), pltpu.SemaphoreType.DMA(())],
        compiler_params=_cparams(("arbitrary",)),
        name="dispatch_gather",
    )(idx.reshape(steps, 1, rows), src)


def _combine_kernel(i0_ref, i1_ref, y_hbm, h_ref, gate_ref, g_ref, b_ref, o_ref, buf0, buf1, sem, *, rows):
    def issue(r, _):
        _row_copy(y_hbm, i0_ref[0, 0, r], buf0, r, sem).start()
        _row_copy(y_hbm, i1_ref[0, 0, r], buf1, r, sem).start()
        return 0

    lax.fori_loop(0, rows, issue, 0, unroll=8)

    def drain(r, _):
        _row_copy(y_hbm, 0, buf0, r, sem).wait()
        _row_copy(y_hbm, 0, buf1, r, sem).wait()
        return 0

    lax.fori_loop(0, rows, drain, 0)
    gate = gate_ref[...]
    y = buf0[...] * gate[:, 0:1] + buf1[...] * gate[:, 1:2]
    o_ref[...] = _layer_norm(DN_ALPHA * h_ref[...] + y, g_ref[...], b_ref[...])


def _combine(y, slot0, slot1, h, gate, g, b, batch, seq, lp, rows=GATHER_ROWS):
    per_batch = seq // rows
    lead = ROW_TILE // rows
    lp_blocks = lp // rows
    hrow = lambda w: pl.BlockSpec((rows, w), lambda bi, j: (bi * lp_blocks + lead + j, 0))
    islot = pl.BlockSpec((1, 1, rows), lambda bi, j: (bi * per_batch + j, 0, 0), memory_space=pltpu.SMEM)
    vec = pl.BlockSpec((1, D_MODEL), lambda bi, j: (0, 0))
    steps = batch * per_batch
    return pl.pallas_call(
        functools.partial(_combine_kernel, rows=rows),
        grid=(batch, per_batch),
        in_specs=[islot, islot, pl.BlockSpec(memory_space=pl.ANY), hrow(D_MODEL), hrow(LANE), vec, vec],
        out_specs=pl.BlockSpec((rows, D_MODEL), lambda bi, j: (bi * per_batch + j, 0)),
        out_shape=jax.ShapeDtypeStruct((batch * seq, D_MODEL), F32),
        scratch_shapes=[pltpu.VMEM((rows, D_MODEL), F32), pltpu.VMEM((rows, D_MODEL), F32),
                        pltpu.SemaphoreType.DMA(())],
        compiler_params=_cparams(("arbitrary", "arbitrary")),
        name="combine_ln",
    )(slot0.reshape(steps, 1, rows), slot1.reshape(steps, 1, rows), y, h, gate, g, b)


def _rot_cols(w):
    half = MLA_ROPE_DIM // 2
    return jnp.concatenate([-w[..., half:], w[..., :half]], axis=-1)


def _rope_table(batch, lp):
    half = MLA_ROPE_DIM // 2
    inv = ROPE_THETA ** (-jnp.arange(half, dtype=F32) * 2.0 / MLA_ROPE_DIM)
    pos = (jnp.arange(lp) - FRONT).astype(F32)
    ang = pos[:, None] * inv[None, :]
    z = jnp.zeros((lp, LANE - MLA_ROPE_DIM), F32)
    cos = jnp.concatenate([jnp.cos(ang), jnp.cos(ang), z], axis=-1)
    sin = jnp.concatenate([jnp.sin(ang), jnp.sin(ang), z], axis=-1)
    return jnp.tile(jnp.concatenate([cos, sin], axis=-1), (batch, 1))


def _pad_cols(w, width):
    return jnp.pad(w, ((0, 0), (0, width - w.shape[-1])))


def _routing(top_idx, batch, seq, lp):
    t = batch * seq
    a = t * TOP_K
    real = top_idx.reshape(batch, lp, LANE)[:, ROW_TILE:, :TOP_K]
    flat_e = real.reshape(a)
    tok_row = (jnp.arange(batch)[:, None] * lp + ROW_TILE + jnp.arange(seq)[None, :]).reshape(t)
    flat_row = jnp.repeat(tok_row, TOP_K).astype(jnp.int32)
    onehot = (flat_e[:, None] == jnp.arange(N_EXPERTS)[None, :]).astype(jnp.int32)
    csum = jnp.cumsum(onehot, axis=0)
    rank = jnp.take_along_axis(csum, flat_e[:, None], axis=1)[:, 0] - 1
    counts = csum[-1]
    padded = (counts + MOE_ROWS - 1) // MOE_ROWS * MOE_ROWS
    pend = jnp.cumsum(padded)
    pstart = pend - padded
    dest = (pstart[flat_e] + rank).astype(jnp.int32)
    nblk = a // MOE_ROWS + N_EXPERTS
    p = nblk * MOE_ROWS
    src = jnp.zeros((p,), jnp.int32).at[dest].set(flat_row)
    blk_e = jnp.minimum(jnp.sum(jnp.arange(nblk)[:, None] * MOE_ROWS >= pend[None, :], -1),
                        N_EXPERTS - 1).astype(jnp.int32)
    n_used = (pend[-1:] // MOE_ROWS).astype(jnp.int32)
    dest2 = dest.reshape(t, TOP_K)
    return src, blk_e, n_used, dest2[:, 0], dest2[:, 1]


def kernel(x, meta_tokens, ev_w_in, ev_sinks, ev_g_cq, ev_w_uq, ev_g_ckv, ev_w_ukv, ev_w_o, ev_ln1_g, ev_ln1_b, ev_ffn_w_gate, ev_ffn_w_up, ev_ffn_w_down, ev_ln2_g, ev_ln2_b, od_w_pw1, od_b_pw1, od_w_dw, od_b_dw, od_g_cn, od_b_cn, od_w_pw2, od_b_pw2, od_ln1_g, od_ln1_b, od_w_router, od_b_router, od_moe_w_gate, od_moe_w_up, od_moe_w_down, od_ln2_g, od_ln2_b):
    batch, seq, _ = x.shape
    assert seq % ROW_TILE == 0
    lp = ROW_TILE + seq
    tp = batch * lp
    vec = lambda v: v.reshape(1, -1).astype(F32)

    meta = jnp.broadcast_to(meta_tokens[None].astype(x.dtype), (batch, N_META, D_MODEL))
    h = jnp.concatenate([jnp.zeros((batch, FRONT, D_MODEL), x.dtype), meta, x], axis=1).reshape(tp, D_MODEL)
    cs = _rope_table(batch, lp)

    w_in = ev_w_in[0]
    o = SWA_Q_W + 2 * SWA_KV_W + MLA_Q_RANK + MLA_KV_RANK
    w_kr = w_in[:, o:]
    w_in_all = jnp.concatenate([w_in[:, :SWA_Q_W] * (SWA_HEAD_DIM ** -0.5), w_in[:, SWA_Q_W:o],
                                _pad_cols(w_kr, LANE), _pad_cols(_rot_cols(w_kr), LANE)], axis=1).astype(BF16)
    aq, akv, cq, ckv, kr = _inproj(h, w_in_all, cs)

    w_uq = ev_w_uq[0].reshape(MLA_Q_RANK, MLA_HEADS, MLA_QK_DIM)
    w_uq_pad = jnp.pad(w_uq, ((0, 0), (0, 0), (0, MLA_PAD_W - MLA_QK_DIM))).reshape(MLA_Q_RANK, -1)
    w_uq_rot = _rot_cols(w_uq[..., MLA_NOPE_DIM:]).reshape(MLA_Q_RANK, -1)
    wqt = jnp.concatenate([w_uq_pad, w_uq_rot], axis=1).T.astype(BF16)
    w_ukv = ev_w_ukv[0].reshape(MLA_KV_RANK, MLA_HEADS, MLA_NOPE_DIM + MLA_V_DIM)
    wk = w_ukv[..., :MLA_NOPE_DIM].reshape(MLA_KV_RANK, -1).astype(BF16)
    wvt = w_ukv[..., MLA_NOPE_DIM:].reshape(MLA_KV_RANK, -1).T.astype(BF16)
    cst = jnp.concatenate([cs[:, :MLA_ROPE_DIM], cs[:, LANE:LANE + MLA_ROPE_DIM]], axis=1).T
    qt, kk, vt = _mla_up(cq, ckv, kr, cst, vec(ev_g_cq[0]), vec(ev_g_ckv[0]), wqt, wk, wvt)

    out_a = _swa(aq.reshape(batch, lp, -1), akv.reshape(batch, lp, -1), ev_sinks[0].astype(F32))
    out_b = _mla(qt, kk.reshape(batch, lp, -1), vt, batch)
    h = _outproj(out_a.reshape(tp, -1), out_b.reshape(tp, -1), h, ev_w_o[0].astype(BF16),
                 vec(ev_ln1_g[0]), vec(ev_ln1_b[0]))

    n_dense = tp // ROW_TILE
    h = _ffn(h, jnp.zeros((n_dense,), jnp.int32), jnp.full((1,), n_dense, jnp.int32),
             ev_ffn_w_gate.astype(BF16), ev_ffn_w_up.astype(BF16),
             ev_ffn_w_down.astype(BF16), ln=(vec(ev_ln2_g[0]), vec(ev_ln2_b[0])))

    u = _pw1(h, od_w_pw1[0].astype(BF16), vec(od_b_pw1[0]))
    wdw = jnp.pad(od_w_dw[0].astype(F32), ((0, HALO - CONV_WIDTH), (0, 0)))
    h, top_idx, gates = _conv_module(
        u, h, wdw, vec(od_b_dw[0]), vec(od_g_cn[0]), vec(od_b_cn[0]),
        od_w_pw2[0].astype(BF16), vec(od_b_pw2[0]), vec(od_ln1_g[0]), vec(od_ln1_b[0]),
        od_w_router[0].astype(F32).T, od_b_router[0].astype(F32),
        tiles_per_batch=lp // ROW_TILE)

    src, blk_e, n_used, slot0, slot1 = _routing(top_idx, batch, seq, lp)
    xg = _gather_rows(h, src)
    y = _ffn(xg, blk_e, n_used, od_moe_w_gate[0].astype(BF16), od_moe_w_up[0].astype(BF16),
             od_moe_w_down[0].astype(BF16), tm=MOE_ROWS)
    out = _combine(y, slot0, slot1, h, gates, vec(od_ln2_g[0]), vec(od_ln2_b[0]), batch, seq, lp)
    return out.reshape(batch, seq, D_MODEL)
```

```python
import functools

import jax
import jax.numpy as jnp
from jax import lax
from jax.experimental import pallas as pl
from jax.experimental.pallas import tpu as pltpu

F32 = jnp.float32
BF16 = jnp.bfloat16

D_MODEL = 2048
DEPTH = 2
N_META = 16
BLOCK = 128
SWA_HEADS = 16
SWA_KV_HEADS = 2
SWA_HEAD_DIM = 64
SWA_GROUP = SWA_HEADS // SWA_KV_HEADS
MLA_HEADS = 8
MLA_Q_RANK = 768
MLA_KV_RANK = 512
MLA_NOPE_DIM = 128
MLA_ROPE_DIM = 64
MLA_V_DIM = 128
MLA_QK_DIM = MLA_NOPE_DIM + MLA_ROPE_DIM
ROPE_THETA = 10000.0
SWA_Q_W = SWA_HEADS * SWA_HEAD_DIM
SWA_KV_W = SWA_KV_HEADS * SWA_HEAD_DIM
CONV_WIDTH = 31
D_FF = 7 * D_MODEL // 2
N_EXPERTS = 8
TOP_K = 2
DN_ALPHA = (2 * DEPTH) ** 0.25
NEG_INF = -1e30
LN_EPS = 1e-5
RMS_EPS = 1e-6

LANE = 128
SUBLANE = 8
VMEM_LIMIT = 56 * 1024 * 1024

ROW_TILE = 512
FRONT = ROW_TILE - N_META
MLA_PAD_W = 256
FF_TILE = 1024
MOE_ROWS = 512
GATHER_ROWS = 256
HALO = 32


def _cparams(sem):
    return pltpu.CompilerParams(dimension_semantics=sem, vmem_limit_bytes=VMEM_LIMIT)


def _layer_norm(x, g, b):
    mu = jnp.mean(x, axis=-1, keepdims=True)
    xc = x - mu
    var = jnp.mean(xc * xc, axis=-1, keepdims=True)
    return xc * lax.rsqrt(var + LN_EPS) * g + b


def _rms_norm(x, g):
    return x * lax.rsqrt(jnp.mean(x * x, axis=-1, keepdims=True) + RMS_EPS) * g


def _sigmoid(x):
    return 1.0 / (1.0 + jnp.exp(-x))


def _dot(a, b):
    return jnp.dot(a, b, preferred_element_type=F32)


def _dot_nt(a, b):
    return lax.dot_general(a, b, (((1,), (1,)), ((), ())), preferred_element_type=F32)


SWA_K_PAD_W = SWA_KV_HEADS * LANE
IN_ROW_W = SWA_K_PAD_W + MLA_Q_RANK + MLA_KV_RANK + 2 * LANE
IN_T_ROWS = SWA_Q_W + SWA_KV_W


def _inproj_kernel(x_ref, w_ref, wt_ref, cs_ref, aqt_ref, ak_ref, avt_ref, cq_ref, ckv_ref, kr_ref):
    xb = x_ref[...].astype(BF16)
    p = _dot(xb, w_ref[...])
    o = SWA_K_PAD_W
    ak_ref[...] = p[:, :o].astype(BF16)
    cq_ref[...] = p[:, o:o + MLA_Q_RANK]
    o += MLA_Q_RANK
    ckv_ref[...] = p[:, o:o + MLA_KV_RANK]
    o += MLA_KV_RANK
    kr = p[:, o:o + LANE]
    kr_rot = p[:, o + LANE:o + 2 * LANE]
    kr_ref[...] = (kr * cs_ref[:, :LANE] + kr_rot * cs_ref[:, LANE:]).astype(BF16)
    pt = _dot_nt(wt_ref[...], xb)
    aqt_ref[...] = pt[:SWA_Q_W, :].astype(BF16)
    avt_ref[...] = pt[SWA_Q_W:, :].astype(BF16)


def _inproj(h, w_row, w_t, cs, tm=256):
    tp = h.shape[0]
    row = lambda w: pl.BlockSpec((tm, w), lambda i: (i, 0))
    col = lambda r: pl.BlockSpec((r, tm), lambda i: (0, i))
    full = lambda a, b: pl.BlockSpec((a, b), lambda i: (0, 0))
    return pl.pallas_call(
        _inproj_kernel,
        grid=(tp // tm,),
        in_specs=[row(D_MODEL), full(D_MODEL, IN_ROW_W), full(IN_T_ROWS, D_MODEL), row(2 * LANE)],
        out_specs=[col(SWA_Q_W), row(SWA_K_PAD_W), col(SWA_KV_W),
                   row(MLA_Q_RANK), row(MLA_KV_RANK), row(LANE)],
        out_shape=[jax.ShapeDtypeStruct((SWA_Q_W, tp), BF16),
                   jax.ShapeDtypeStruct((tp, SWA_K_PAD_W), BF16),
                   jax.ShapeDtypeStruct((SWA_KV_W, tp), BF16),
                   jax.ShapeDtypeStruct((tp, MLA_Q_RANK), F32),
                   jax.ShapeDtypeStruct((tp, MLA_KV_RANK), F32),
                   jax.ShapeDtypeStruct((tp, LANE), BF16)],
        compiler_params=_cparams(("parallel",)),
        name="inproj",
    )(h, w_row, w_t, cs)


MLA_QT_ROWS = MLA_HEADS * MLA_PAD_W + MLA_HEADS * MLA_ROPE_DIM
LOG2E = 1.4426950408889634


def _mla_up_kernel(cq_ref, ckv_ref, kr_ref, cst_ref, gq_ref, gkv_ref, wqt_ref, wk_ref, wvt_ref,
                   qt_ref, kk_ref, vt_ref):
    scale = MLA_QK_DIM ** -0.5 * LOG2E
    tm = cq_ref.shape[0]
    cos = cst_ref[:MLA_ROPE_DIM, :]
    sin = cst_ref[MLA_ROPE_DIM:, :]
    cqn = _rms_norm(cq_ref[...], gq_ref[...]).astype(BF16)
    qa = _dot_nt(wqt_ref[...], cqn)
    rot0 = MLA_HEADS * MLA_PAD_W
    for h in range(MLA_HEADS):
        a = h * MLA_PAD_W
        r = a + MLA_NOPE_DIM
        qt_ref[a:r, :] = (qa[a:r, :] * scale).astype(BF16)
        roped = (qa[r:r + MLA_ROPE_DIM, :] * cos
                 + qa[rot0 + h * MLA_ROPE_DIM:rot0 + (h + 1) * MLA_ROPE_DIM, :] * sin)
        qt_ref[r:r + MLA_ROPE_DIM, :] = (roped * scale).astype(BF16)
        qt_ref[r + MLA_ROPE_DIM:a + MLA_PAD_W, :] = jnp.zeros((MLA_PAD_W - MLA_QK_DIM, tm), BF16)
    ckvn = _rms_norm(ckv_ref[...], gkv_ref[...]).astype(BF16)
    kn = _dot(ckvn, wk_ref[...])
    kr = kr_ref[...]
    for h in range(MLA_HEADS):
        kk_ref[:, h * MLA_PAD_W:h * MLA_PAD_W + LANE] = kn[:, h * MLA_NOPE_DIM:(h + 1) * MLA_NOPE_DIM].astype(BF16)
        kk_ref[:, h * MLA_PAD_W + LANE:(h + 1) * MLA_PAD_W] = kr
    vt_ref[0] = _dot_nt(wvt_ref[...], ckvn).astype(BF16)


def _mla_up(cq, ckv, kr, cst, gq, gkv, wqt, wk, wvt, tm=ROW_TILE):
    tp = cq.shape[0]
    row = lambda w: pl.BlockSpec((tm, w), lambda i: (i, 0))
    col = lambda r: pl.BlockSpec((r, tm), lambda i: (0, i))
    full = lambda a, b: pl.BlockSpec((a, b), lambda i: (0, 0))
    kw = MLA_HEADS * MLA_PAD_W
    vw = MLA_HEADS * MLA_V_DIM
    return pl.pallas_call(
        _mla_up_kernel,
        grid=(tp // tm,),
        in_specs=[row(MLA_Q_RANK), row(MLA_KV_RANK), row(LANE), col(2 * MLA_ROPE_DIM),
                  full(1, MLA_Q_RANK), full(1, MLA_KV_RANK),
                  full(MLA_QT_ROWS, MLA_Q_RANK), full(MLA_KV_RANK, MLA_HEADS * MLA_NOPE_DIM),
                  full(vw, MLA_KV_RANK)],
        out_specs=[col(kw), row(kw), pl.BlockSpec((1, vw, tm), lambda i: (i, 0, 0))],
        out_shape=[jax.ShapeDtypeStruct((kw, tp), BF16),
                   jax.ShapeDtypeStruct((tp, kw), BF16),
                   jax.ShapeDtypeStruct((tp // tm, vw, tm), BF16)],
        compiler_params=_cparams(("parallel",)),
        name="mla_up",
    )(cq, ckv, kr, cst, gq, gkv, wqt, wk, wvt)


META_BLOCK = FRONT // BLOCK
META_ROW0 = FRONT % BLOCK


def _swa_kernel(sink_ref, qt_ref, kc_ref, kp_ref, km_ref, vc_ref, vp_ref, vm_ref, o_ref):
    n = pl.program_id(1) - META_BLOCK
    r = lax.broadcasted_iota(jnp.int32, (3 * BLOCK, BLOCK), 0)
    i = lax.broadcasted_iota(jnp.int32, (3 * BLOCK, BLOCK), 1)
    never = 4 * BLOCK
    ok_meta = (r >= META_ROW0) & (r <= n * BLOCK + i)
    ok_prev = (r - BLOCK) > (i + jnp.where(n >= 2, 0, never))
    ok_cur = (r - 2 * BLOCK + jnp.where(n >= 1, 0, never)) <= i
    ok = ((r < BLOCK) & ok_meta) | ((r >= BLOCK) & (r < 2 * BLOCK) & ok_prev) | ((r >= 2 * BLOCK) & ok_cur)
    bias = jnp.where(ok, 0.0, NEG_INF)
    bias = jnp.concatenate([bias] * SWA_GROUP, axis=1)
    km, kp, kc = km_ref[...], kp_ref[...], kc_ref[...]
    outs = []
    for g in range(SWA_KV_HEADS):
        ks = slice(g * LANE, g * LANE + SWA_HEAD_DIM)
        ds = slice(g * SWA_HEAD_DIM, (g + 1) * SWA_HEAD_DIM)
        k = jnp.concatenate([km[:, ks], kp[:, ks], kc[:, ks]], axis=0)
        vt = jnp.concatenate([vm_ref[ds, :], vp_ref[ds, :], vc_ref[ds, :]], axis=1)
        qt = jnp.concatenate([qt_ref[(g * SWA_GROUP + hh) * SWA_HEAD_DIM:(g * SWA_GROUP + hh + 1) * SWA_HEAD_DIM, :]
                              for hh in range(SWA_GROUP)], axis=1)
        s = _dot(k, qt) + bias
        sink = sink_ref[g:g + 1, :]
        m = jnp.maximum(jnp.max(s, axis=0, keepdims=True), sink)
        p = jnp.exp2(s - m)
        den = jnp.sum(p, axis=0, keepdims=True) + jnp.exp2(sink - m)
        o = _dot(vt, p.astype(BF16)) / den
        outs += [o[:, hh * BLOCK:(hh + 1) * BLOCK] for hh in range(SWA_GROUP)]
    o_ref[0] = jnp.concatenate(outs, axis=0).T.astype(BF16)


def _swa(aqt, ak, avt, sink_rows, batch):
    tp = ak.shape[0]
    nb = tp // batch // BLOCK
    cur = lambda bi, n: bi * nb + n
    prev = lambda bi, n: bi * nb + jnp.maximum(n - 1, 0)
    meta = lambda bi, n: bi * nb + META_BLOCK
    kspec = lambda f: pl.BlockSpec((BLOCK, SWA_K_PAD_W), lambda bi, n: (f(bi, n), 0))
    vspec = lambda f: pl.BlockSpec((SWA_KV_W, BLOCK), lambda bi, n: (0, f(bi, n)))
    return pl.pallas_call(
        _swa_kernel,
        grid=(batch, nb),
        in_specs=[pl.BlockSpec((SWA_KV_HEADS, SWA_GROUP * BLOCK), lambda bi, n: (0, 0)),
                  pl.BlockSpec((SWA_Q_W, BLOCK), lambda bi, n: (0, cur(bi, n))),
                  kspec(cur), kspec(prev), kspec(meta), vspec(cur), vspec(prev), vspec(meta)],
        out_specs=pl.BlockSpec((1, BLOCK, SWA_Q_W), lambda bi, n: (bi, n, 0)),
        out_shape=jax.ShapeDtypeStruct((batch, nb * BLOCK, SWA_Q_W), BF16),
        compiler_params=_cparams(("parallel", "parallel")),
        name="swa",
    )(sink_rows, aqt, ak, ak, ak, avt, avt, avt)


def _mla_kernel(qt_ref, k_ref, vt_ref, o_ref, sa_scr, sb_scr, *, tq):
    qi = pl.program_id(2)
    qt = qt_ref[...]
    meta0 = FRONT // LANE * LANE

    def scores(ki):
        start = pl.multiple_of(ki * tq, tq)
        return _dot(k_ref[0, pl.ds(start, tq), :], qt)

    def masked(s, row0):
        kidx = row0 + lax.broadcasted_iota(jnp.int32, s.shape, 0)
        qidx = qi * tq + lax.broadcasted_iota(jnp.int32, s.shape, 1)
        return jnp.where((kidx <= qidx) & (kidx >= FRONT), s, NEG_INF)

    def update(s, smax, vb, carry):
        m, l, acc = carry
        m_new = jnp.maximum(m, smax)
        alpha = jnp.exp2(m - m_new)
        p = jnp.exp2(s - m_new)
        l = alpha * l + jnp.sum(p, axis=0, keepdims=True)
        acc = alpha * acc + _dot(vb, p.astype(BF16))
        return m_new, l, acc

    def update_masked(s, row0, vb, carry):
        s = masked(s, row0)
        return update(s, jnp.max(s, axis=0, keepdims=True), vb, carry)

    init = (jnp.full((1, tq), NEG_INF, F32), jnp.zeros((1, tq), F32), jnp.zeros((MLA_V_DIM, tq), F32))
    carry = update_masked(_dot(k_ref[0, meta0:tq, :], qt), meta0, vt_ref[0, :, meta0:tq], init)

    def issue(ki, dst):
        s = scores(ki)
        dst[...] = s
        return jnp.max(s, axis=0, keepdims=True)

    def consume(src, smax, ki, carry):
        return update(src[...], smax, vt_ref[ki], carry)

    def diagonal(src, carry):
        return update_masked(src[...], qi * tq, vt_ref[qi], carry)

    def tiles_after_first(carry):
        n_full = qi - 1

        def pair(t, c):
            smax_a, carry = c
            ki = 1 + 2 * t
            smax_b = issue(ki + 1, sb_scr)
            carry = consume(sa_scr, smax_a, ki, carry)
            smax_a = issue(ki + 2, sa_scr)
            carry = consume(sb_scr, smax_b, ki + 1, carry)
            return smax_a, carry

        smax_a, carry = lax.fori_loop(0, n_full // 2, pair, (issue(1, sa_scr), carry))

        def odd_tail(carry):
            issue(qi, sb_scr)
            return diagonal(sb_scr, consume(sa_scr, smax_a, qi - 1, carry))

        return lax.cond(n_full % 2 == 1, odd_tail, lambda c: diagonal(sa_scr, c), carry)

    _, l, acc = lax.cond(qi > 0, tiles_after_first, lambda c: c, carry)
    o_ref[0] = (acc / l).T.astype(BF16)


def _mla(qt, kk, vt, batch, tq=ROW_TILE):
    lp = kk.shape[1]
    nq = lp // tq
    return pl.pallas_call(
        functools.partial(_mla_kernel, tq=tq),
        grid=(batch, MLA_HEADS, nq),
        in_specs=[pl.BlockSpec((MLA_PAD_W, tq), lambda bi, h, i: (h, bi * nq + i)),
                  pl.BlockSpec((1, lp, MLA_PAD_W), lambda bi, h, i: (bi, 0, h)),
                  pl.BlockSpec((nq, MLA_V_DIM, tq), lambda bi, h, i: (bi, h, 0))],
        out_specs=pl.BlockSpec((1, tq, MLA_V_DIM), lambda bi, h, i: (bi, i, h)),
        out_shape=jax.ShapeDtypeStruct((batch, lp, MLA_HEADS * MLA_V_DIM), BF16),
        scratch_shapes=[pltpu.VMEM((tq, tq), F32), pltpu.VMEM((tq, tq), F32)],
        compiler_params=_cparams(("parallel", "parallel", "arbitrary")),
        name="mla",
    )(qt, kk, vt)


def _outproj_kernel(oa_ref, ob_ref, h_ref, w_ref, g_ref, b_ref, o_ref):
    mix = _dot(oa_ref[...], w_ref[:SWA_Q_W, :]) + _dot(ob_ref[...], w_ref[SWA_Q_W:, :])
    o_ref[...] = _layer_norm(DN_ALPHA * h_ref[...] + mix, g_ref[...], b_ref[...])


def _outproj(oa, ob, h, w_o, g, b, tm=ROW_TILE):
    tp = h.shape[0]
    row = lambda w: pl.BlockSpec((tm, w), lambda i: (i, 0))
    full = lambda a, c: pl.BlockSpec((a, c), lambda i: (0, 0))
    return pl.pallas_call(
        _outproj_kernel,
        grid=(tp // tm,),
        in_specs=[row(SWA_Q_W), row(MLA_HEADS * MLA_V_DIM), row(D_MODEL),
                  full(D_MODEL, D_MODEL), full(1, D_MODEL), full(1, D_MODEL)],
        out_specs=row(D_MODEL),
        out_shape=jax.ShapeDtypeStruct((tp, D_MODEL), F32),
        compiler_params=_cparams(("parallel",)),
        name="outproj_ln",
    )(oa, ob, h, w_o, g, b)


def _swiglu_step(xb, wg_ref, wu_ref, wd_ref, acc_scr):
    a = _dot(xb, wg_ref[0])
    u = _dot(xb, wu_ref[0])
    act = (a * _sigmoid(a) * u).astype(BF16)
    acc_scr[...] += _dot(act, wd_ref[0])


def _ffn_dense_kernel(be_ref, nu_ref, x_ref, wg_ref, wu_ref, wd_ref, g_ref, b_ref, o_ref, xb_scr, acc_scr):
    j = pl.program_id(1)

    @pl.when(j == 0)
    def _():
        xb_scr[...] = x_ref[...].astype(BF16)
        acc_scr[...] = jnp.zeros_like(acc_scr)

    _swiglu_step(xb_scr[...], wg_ref, wu_ref, wd_ref, acc_scr)

    @pl.when(j == pl.num_programs(1) - 1)
    def _():
        o_ref[...] = _layer_norm(DN_ALPHA * x_ref[...] + acc_scr[...], g_ref[...], b_ref[...])


def _ffn_routed_kernel(be_ref, nu_ref, x_ref, wg_ref, wu_ref, wd_ref, o_ref, acc_scr):
    j = pl.program_id(1)

    @pl.when(pl.program_id(0) < nu_ref[0])
    def _():
        @pl.when(j == 0)
        def _():
            acc_scr[...] = jnp.zeros_like(acc_scr)

        _swiglu_step(x_ref[...], wg_ref, wu_ref, wd_ref, acc_scr)

        @pl.when(j == pl.num_programs(1) - 1)
        def _():
            o_ref[...] = acc_scr[...]

    @pl.when((pl.program_id(0) >= nu_ref[0]) & (j == pl.num_programs(1) - 1))
    def _():
        o_ref[...] = jnp.zeros_like(o_ref)


def _ffn(x, block_expert, n_used, w_gate, w_up, w_down, ln=None, tm=ROW_TILE, tf=FF_TILE):
    rows = x.shape[0]
    nj = D_FF // tf
    blk = lambda i, nu: jnp.minimum(i, nu[0] - 1)
    ff = lambda i, j, nu: jnp.where(i < nu[0], j, nj - 1)
    row = pl.BlockSpec((tm, D_MODEL), lambda i, j, be, nu: (blk(i, nu), 0))
    in_specs = [row,
                pl.BlockSpec((1, D_MODEL, tf), lambda i, j, be, nu: (be[blk(i, nu)], 0, ff(i, j, nu))),
                pl.BlockSpec((1, D_MODEL, tf), lambda i, j, be, nu: (be[blk(i, nu)], 0, ff(i, j, nu))),
                pl.BlockSpec((1, tf, D_MODEL), lambda i, j, be, nu: (be[blk(i, nu)], ff(i, j, nu), 0))]
    scratch = [pltpu.VMEM((tm, D_MODEL), F32)]
    args = [block_expert, n_used, x, w_gate, w_up, w_down]
    if ln is not None:
        vec = pl.BlockSpec((1, D_MODEL), lambda i, j, be, nu: (0, 0))
        in_specs += [vec, vec]
        scratch = [pltpu.VMEM((tm, D_MODEL), BF16)] + scratch
        args += list(ln)
        body, name = _ffn_dense_kernel, "ffn_dense_ln"
    else:
        body, name = _ffn_routed_kernel, "ffn_routed"
    return pl.pallas_call(
        body,
        grid_spec=pltpu.PrefetchScalarGridSpec(
            num_scalar_prefetch=2,
            grid=(rows // tm, nj),
            in_specs=in_specs,
            out_specs=pl.BlockSpec((tm, D_MODEL), lambda i, j, be, nu: (i, 0)),
            scratch_shapes=scratch),
        out_shape=jax.ShapeDtypeStruct((rows, D_MODEL), F32),
        compiler_params=_cparams(("arbitrary", "arbitrary")),
        name=name,
    )(*args)


def _pw1_kernel(h_ref, wa_ref, wg_ref, ba_ref, bg_ref, u_ref, xb_scr):
    @pl.when(pl.program_id(1) == 0)
    def _():
        xb_scr[...] = h_ref[...].astype(BF16)

    xb = xb_scr[...]
    a = _dot(xb, wa_ref[...]) + ba_ref[...]
    g = _dot(xb, wg_ref[...]) + bg_ref[...]
    u_ref[...] = a * _sigmoid(g)


def _pw1(h, w_pw1, b_pw1, tm=ROW_TILE, tn=1024):
    tp = h.shape[0]
    nj = D_MODEL // tn
    return pl.pallas_call(
        _pw1_kernel,
        grid=(tp // tm, nj),
        in_specs=[pl.BlockSpec((tm, D_MODEL), lambda i, j: (i, 0)),
                  pl.BlockSpec((D_MODEL, tn), lambda i, j: (0, j)),
                  pl.BlockSpec((D_MODEL, tn), lambda i, j: (0, j + nj)),
                  pl.BlockSpec((1, tn), lambda i, j: (0, j)),
                  pl.BlockSpec((1, tn), lambda i, j: (0, j + nj))],
        out_specs=pl.BlockSpec((tm, tn), lambda i, j: (i, j)),
        out_shape=jax.ShapeDtypeStruct((tp, D_MODEL), F32),
        scratch_shapes=[pltpu.VMEM((tm, D_MODEL), BF16)],
        compiler_params=_cparams(("parallel", "arbitrary")),
        name="pw1_glu",
    )(h, w_pw1, w_pw1, b_pw1, b_pw1)


CONV_ROWS = 32
CONV_COLS = 256


def _conv_kernel(uc_ref, up_ref, h_ref, wdw_ref, bdw_ref, gcn_ref, bcn_ref, w2_ref, b2_ref,
                 g1_ref, b1_ref, wr_ref, br_ref, ho_ref, idx_ref, gate_ref, win_scr, cv_scr,
                 *, tm, tiles_per_batch):
    row0 = (pl.program_id(0) % tiles_per_batch) * tm
    win_scr[HALO:, :] = uc_ref[...]
    win_scr[:HALO, :] = up_ref[...]

    @pl.when(row0 < FRONT)
    def _():
        r = lax.broadcasted_iota(jnp.int32, (tm + HALO, 1), 0) + (row0 - HALO)
        win_scr[...] = jnp.where(r >= FRONT, win_scr[...], 0.0)

    off = HALO - (CONV_WIDTH - 1)

    def rows_body(rc, _):
        r0 = pl.multiple_of(rc * CONV_ROWS, CONV_ROWS)
        for cc in range(D_MODEL // CONV_COLS):
            cs = slice(cc * CONV_COLS, (cc + 1) * CONV_COLS)
            blk = win_scr[pl.ds(r0, CONV_ROWS + HALO), cs]
            acc = jnp.zeros((CONV_ROWS, CONV_COLS), F32)
            for b in range(SUBLANE):
                ub = pltpu.roll(blk, CONV_ROWS + HALO - (off + b), axis=0)
                for a in range((CONV_WIDTH - 1 - b) // SUBLANE + 1):
                    k = SUBLANE * a + b
                    acc = acc + ub[SUBLANE * a:SUBLANE * a + CONV_ROWS, :] * wdw_ref[k:k + 1, cs]
            cv_scr[pl.ds(r0, CONV_ROWS), cs] = acc
        return 0

    lax.fori_loop(0, tm // CONV_ROWS, rows_body, 0)

    c = _layer_norm(cv_scr[...] + bdw_ref[...], gcn_ref[...], bcn_ref[...])
    c = c * _sigmoid(c)
    mix = _dot(c.astype(BF16), w2_ref[...]) + b2_ref[...]
    hn = _layer_norm(DN_ALPHA * h_ref[...] + mix, g1_ref[...], b1_ref[...])
    ho_ref[...] = hn

    lg = [jnp.sum(hn * wr_ref[e:e + 1, :], axis=-1, keepdims=True) + br_ref[e] for e in range(N_EXPERTS)]

    def top1(vals):
        best = vals[0]
        for v in vals[1:]:
            best = jnp.maximum(best, v)
        idx = jnp.full(best.shape, N_EXPERTS - 1, jnp.int32)
        for e in range(N_EXPERTS - 2, -1, -1):
            idx = jnp.where(vals[e] == best, e, idx)
        return best, idx

    m1, i1 = top1(lg)
    m2, i2 = top1([jnp.where(i1 == e, -jnp.inf, lg[e]) for e in range(N_EXPERTS)])
    lane = lax.broadcasted_iota(jnp.int32, (hn.shape[0], LANE), 1)
    e = jnp.exp(m2 - m1)
    gate1 = 1.0 / (1.0 + e)
    gate2 = e * gate1
    idx_ref[...] = jnp.where(lane == 0, i1, jnp.where(lane == 1, i2, 0))
    gate_ref[...] = jnp.where(lane == 0, gate1, jnp.where(lane == 1, gate2, 0.0))


def _conv_module(u, h, wdw, bdw, gcn, bcn, w2, b2, g1, b1, wr, br, tiles_per_batch, tm=256):
    tp = h.shape[0]
    row = lambda w: pl.BlockSpec((tm, w), lambda i: (i, 0))
    full = lambda a, c: pl.BlockSpec((a, c), lambda i: (0, 0))
    halo_blocks = tm // HALO
    return pl.pallas_call(
        functools.partial(_conv_kernel, tm=tm, tiles_per_batch=tiles_per_batch * (ROW_TILE // tm)),
        grid=(tp // tm,),
        in_specs=[row(D_MODEL),
                  pl.BlockSpec((HALO, D_MODEL), lambda i: (jnp.maximum(i * halo_blocks - 1, 0), 0)),
                  row(D_MODEL),
                  full(HALO, D_MODEL), full(1, D_MODEL), full(1, D_MODEL), full(1, D_MODEL),
                  full(D_MODEL, D_MODEL), full(1, D_MODEL), full(1, D_MODEL), full(1, D_MODEL),
                  full(N_EXPERTS, D_MODEL), pl.BlockSpec(memory_space=pltpu.SMEM)],
        out_specs=[row(D_MODEL), row(LANE), row(LANE)],
        out_shape=[jax.ShapeDtypeStruct((tp, D_MODEL), F32),
                   jax.ShapeDtypeStruct((tp, LANE), jnp.int32),
                   jax.ShapeDtypeStruct((tp, LANE), F32)],
        scratch_shapes=[pltpu.VMEM((tm + HALO, D_MODEL), F32), pltpu.VMEM((tm, D_MODEL), F32)],
        compiler_params=_cparams(("parallel",)),
        name="conv_module_ln_router",
    )(u, u, h, wdw, bdw, gcn, bcn, w2, b2, g1, b1, wr, br)


def _row_copy(src_hbm, row, buf, slot, sem):
    return pltpu.make_async_copy(src_hbm.at[pl.ds(row, 1), :], buf.at[pl.ds(slot, 1), :], sem)


def _issue_rows(src_hbm, idx_ref, buf, sem, rows):
    def body(r, _):
        _row_copy(src_hbm, idx_ref[0, 0, r], buf, r, sem).start()
        return 0

    lax.fori_loop(0, rows, body, 0, unroll=8)


def _drain_rows(src_hbm, buf, sem, rows):
    def body(r, _):
        _row_copy(src_hbm, 0, buf, r, sem).wait()
        return 0

    lax.fori_loop(0, rows, body, 0, unroll=8)


def _gather_kernel(idx_ref, nxt_ref, src_hbm, o_ref, buf, sem, *, rows):
    i = pl.program_id(0)
    slot = i % 2

    @pl.when(i == 0)
    def _():
        _issue_rows(src_hbm, idx_ref, buf.at[0], sem.at[0], rows)

    @pl.when(i + 1 < pl.num_programs(0))
    def _():
        _issue_rows(src_hbm, nxt_ref, buf.at[1 - slot], sem.at[1 - slot], rows)

    _drain_rows(src_hbm, buf.at[slot], sem.at[slot], rows)
    o_ref[...] = buf[slot].astype(o_ref.dtype)


def _gather_rows(src, idx, rows=GATHER_ROWS):
    n = idx.shape[0]
    steps = n // rows
    islot = lambda f: pl.BlockSpec((1, 1, rows), lambda i: (f(i), 0, 0), memory_space=pltpu.SMEM)
    idx3 = idx.reshape(steps, 1, rows)
    return pl.pallas_call(
        functools.partial(_gather_kernel, rows=rows),
        grid=(steps,),
        in_specs=[islot(lambda i: i), islot(lambda i: jnp.minimum(i + 1, steps - 1)),
                  pl.BlockSpec(memory_space=pl.ANY)],
        out_specs=pl.BlockSpec((rows, D_MODEL), lambda i: (i, 0)),
        out_shape=jax.ShapeDtypeStruct((n, D_MODEL), BF16),
        scratch_shapes=[pltpu.VMEM((2, rows, D_MODEL), F32), pltpu.SemaphoreType.DMA((2,))],
        compiler_params=_cparams(("arbitrary",)),
        name="dispatch_gather",
    )(idx3, idx3, src)


def _combine_kernel(i0_ref, i1_ref, n0_ref, n1_ref, y_hbm, h_ref, gate_ref, g_ref, b_ref, o_ref,
                    buf0, buf1, sem, *, rows):
    t = pl.program_id(0) * pl.num_programs(1) + pl.program_id(1)
    steps = pl.num_programs(0) * pl.num_programs(1)
    slot = t % 2

    def issue(a_ref, b_ref, s):
        _issue_rows(y_hbm, a_ref, buf0.at[s], sem.at[s], rows)
        _issue_rows(y_hbm, b_ref, buf1.at[s], sem.at[s], rows)

    @pl.when(t == 0)
    def _():
        issue(i0_ref, i1_ref, 0)

    @pl.when(t + 1 < steps)
    def _():
        issue(n0_ref, n1_ref, 1 - slot)

    _drain_rows(y_hbm, buf0.at[slot], sem.at[slot], rows)
    _drain_rows(y_hbm, buf1.at[slot], sem.at[slot], rows)
    gate = gate_ref[...]
    y = buf0[slot] * gate[:, 0:1] + buf1[slot] * gate[:, 1:2]
    o_ref[...] = _layer_norm(DN_ALPHA * h_ref[...] + y, g_ref[...], b_ref[...])


def _combine(y, slot0, slot1, h, gate, g, b, batch, seq, lp, rows=GATHER_ROWS):
    per_batch = seq // rows
    lead = ROW_TILE // rows
    lp_blocks = lp // rows
    steps = batch * per_batch
    hrow = lambda w: pl.BlockSpec((rows, w), lambda bi, j: (bi * lp_blocks + lead + j, 0))
    cur = lambda bi, j: bi * per_batch + j
    nxt = lambda bi, j: jnp.minimum(bi * per_batch + j + 1, steps - 1)
    islot = lambda f: pl.BlockSpec((1, 1, rows), lambda bi, j: (f(bi, j), 0, 0), memory_space=pltpu.SMEM)
    vec = pl.BlockSpec((1, D_MODEL), lambda bi, j: (0, 0))
    s0 = slot0.reshape(steps, 1, rows)
    s1 = slot1.reshape(steps, 1, rows)
    return pl.pallas_call(
        functools.partial(_combine_kernel, rows=rows),
        grid=(batch, per_batch),
        in_specs=[islot(cur), islot(cur), islot(nxt), islot(nxt),
                  pl.BlockSpec(memory_space=pl.ANY), hrow(D_MODEL), hrow(LANE), vec, vec],
        out_specs=pl.BlockSpec((rows, D_MODEL), lambda bi, j: (cur(bi, j), 0)),
        out_shape=jax.ShapeDtypeStruct((batch * seq, D_MODEL), F32),
        scratch_shapes=[pltpu.VMEM((2, rows, D_MODEL), F32), pltpu.VMEM((2, rows, D_MODEL), F32),
                        pltpu.SemaphoreType.DMA((2,))],
        compiler_params=_cparams(("arbitrary", "arbitrary")),
        name="combine_ln",
    )(s0, s1, s0, s1, y, h, gate, g, b)


def _rot_cols(w):
    half = MLA_ROPE_DIM // 2
    return jnp.concatenate([-w[..., half:], w[..., :half]], axis=-1)


def _rope_table(batch, lp):
    half = MLA_ROPE_DIM // 2
    inv = ROPE_THETA ** (-jnp.arange(half, dtype=F32) * 2.0 / MLA_ROPE_DIM)
    pos = (jnp.arange(lp) - FRONT).astype(F32)
    ang = pos[:, None] * inv[None, :]
    z = jnp.zeros((lp, LANE - MLA_ROPE_DIM), F32)
    cos = jnp.concatenate([jnp.cos(ang), jnp.cos(ang), z], axis=-1)
    sin = jnp.concatenate([jnp.sin(ang), jnp.sin(ang), z], axis=-1)
    return jnp.tile(jnp.concatenate([cos, sin], axis=-1), (batch, 1))


def _pad_cols(w, width):
    return jnp.pad(w, ((0, 0), (0, width - w.shape[-1])))


def _routing(top_idx, batch, seq, lp):
    t = batch * seq
    a = t * TOP_K
    real = top_idx.reshape(batch, lp, LANE)[:, ROW_TILE:, :TOP_K]
    flat_e = real.reshape(a)
    tok_row = (jnp.arange(batch)[:, None] * lp + ROW_TILE + jnp.arange(seq)[None, :]).reshape(t)
    flat_row = jnp.repeat(tok_row, TOP_K).astype(jnp.int32)
    onehot = (flat_e[:, None] == jnp.arange(N_EXPERTS)[None, :]).astype(jnp.int32)
    csum = jnp.cumsum(onehot, axis=0)
    rank = jnp.take_along_axis(csum, flat_e[:, None], axis=1)[:, 0] - 1
    counts = csum[-1]
    padded = (counts + MOE_ROWS - 1) // MOE_ROWS * MOE_ROWS
    pend = jnp.cumsum(padded)
    pstart = pend - padded
    dest = (pstart[flat_e] + rank).astype(jnp.int32)
    nblk = a // MOE_ROWS + N_EXPERTS
    p = nblk * MOE_ROWS
    src = jnp.zeros((p,), jnp.int32).at[dest].set(flat_row)
    blk_e = jnp.minimum(jnp.sum(jnp.arange(nblk)[:, None] * MOE_ROWS >= pend[None, :], -1),
                        N_EXPERTS - 1).astype(jnp.int32)
    n_used = (pend[-1:] // MOE_ROWS).astype(jnp.int32)
    dest2 = dest.reshape(t, TOP_K)
    return src, blk_e, n_used, dest2[:, 0], dest2[:, 1]


def kernel(x, meta_tokens, ev_w_in, ev_sinks, ev_g_cq, ev_w_uq, ev_g_ckv, ev_w_ukv, ev_w_o, ev_ln1_g, ev_ln1_b, ev_ffn_w_gate, ev_ffn_w_up, ev_ffn_w_down, ev_ln2_g, ev_ln2_b, od_w_pw1, od_b_pw1, od_w_dw, od_b_dw, od_g_cn, od_b_cn, od_w_pw2, od_b_pw2, od_ln1_g, od_ln1_b, od_w_router, od_b_router, od_moe_w_gate, od_moe_w_up, od_moe_w_down, od_ln2_g, od_ln2_b):
    batch, seq, _ = x.shape
    assert seq % ROW_TILE == 0
    lp = ROW_TILE + seq
    tp = batch * lp
    vec = lambda v: v.reshape(1, -1).astype(F32)

    meta = jnp.broadcast_to(meta_tokens[None].astype(x.dtype), (batch, N_META, D_MODEL))
    h = jnp.concatenate([jnp.zeros((batch, FRONT, D_MODEL), x.dtype), meta, x], axis=1).reshape(tp, D_MODEL)
    cs = _rope_table(batch, lp)

    w_in = ev_w_in[0]
    o = SWA_Q_W + 2 * SWA_KV_W + MLA_Q_RANK + MLA_KV_RANK
    w_kr = w_in[:, o:]
    w_ak = w_in[:, SWA_Q_W:SWA_Q_W + SWA_KV_W].reshape(D_MODEL, SWA_KV_HEADS, SWA_HEAD_DIM)
    w_ak = jnp.pad(w_ak, ((0, 0), (0, 0), (0, LANE - SWA_HEAD_DIM))).reshape(D_MODEL, SWA_K_PAD_W)
    w_row = jnp.concatenate([w_ak, w_in[:, SWA_Q_W + 2 * SWA_KV_W:o],
                             _pad_cols(w_kr, LANE), _pad_cols(_rot_cols(w_kr), LANE)], axis=1).astype(BF16)
    w_t = jnp.concatenate([w_in[:, :SWA_Q_W] * (SWA_HEAD_DIM ** -0.5 * LOG2E),
                           w_in[:, SWA_Q_W + SWA_KV_W:SWA_Q_W + 2 * SWA_KV_W]], axis=1).T.astype(BF16)
    aqt, ak, avt, cq, ckv, kr = _inproj(h, w_row, w_t, cs)

    w_uq = ev_w_uq[0].reshape(MLA_Q_RANK, MLA_HEADS, MLA_QK_DIM)
    w_uq_pad = jnp.pad(w_uq, ((0, 0), (0, 0), (0, MLA_PAD_W - MLA_QK_DIM))).reshape(MLA_Q_RANK, -1)
    w_uq_rot = _rot_cols(w_uq[..., MLA_NOPE_DIM:]).reshape(MLA_Q_RANK, -1)
    wqt = jnp.concatenate([w_uq_pad, w_uq_rot], axis=1).T.astype(BF16)
    w_ukv = ev_w_ukv[0].reshape(MLA_KV_RANK, MLA_HEADS, MLA_NOPE_DIM + MLA_V_DIM)
    wk = w_ukv[..., :MLA_NOPE_DIM].reshape(MLA_KV_RANK, -1).astype(BF16)
    wvt = w_ukv[..., MLA_NOPE_DIM:].reshape(MLA_KV_RANK, -1).T.astype(BF16)
    cst = jnp.concatenate([cs[:, :MLA_ROPE_DIM], cs[:, LANE:LANE + MLA_ROPE_DIM]], axis=1).T
    qt, kk, vt = _mla_up(cq, ckv, kr, cst, vec(ev_g_cq[0]), vec(ev_g_ckv[0]), wqt, wk, wvt)

    sink_rows = jnp.repeat(ev_sinks[0].astype(F32) * LOG2E, BLOCK).reshape(SWA_KV_HEADS, SWA_GROUP * BLOCK)
    out_a = _swa(aqt, ak, avt, sink_rows, batch)
    out_b = _mla(qt, kk.reshape(batch, lp, -1), vt, batch)
    h = _outproj(out_a.reshape(tp, -1), out_b.reshape(tp, -1), h, ev_w_o[0].astype(BF16),
                 vec(ev_ln1_g[0]), vec(ev_ln1_b[0]))

    n_dense = tp // ROW_TILE
    h = _ffn(h, jnp.zeros((n_dense,), jnp.int32), jnp.full((1,), n_dense, jnp.int32),
             ev_ffn_w_gate.astype(BF16), ev_ffn_w_up.astype(BF16),
             ev_ffn_w_down.astype(BF16), ln=(vec(ev_ln2_g[0]), vec(ev_ln2_b[0])))

    u = _pw1(h, od_w_pw1[0].astype(BF16), vec(od_b_pw1[0]))
    wdw = jnp.pad(od_w_dw[0].astype(F32), ((0, HALO - CONV_WIDTH), (0, 0)))
    h, top_idx, gates = _conv_module(
        u, h, wdw, vec(od_b_dw[0]), vec(od_g_cn[0]), vec(od_b_cn[0]),
        od_w_pw2[0].astype(BF16), vec(od_b_pw2[0]), vec(od_ln1_g[0]), vec(od_ln1_b[0]),
        od_w_router[0].astype(F32).T, od_b_router[0].astype(F32),
        tiles_per_batch=lp // ROW_TILE)

    src, blk_e, n_used, slot0, slot1 = _routing(top_idx, batch, seq, lp)
    xg = _gather_rows(h, src)
    y = _ffn(xg, blk_e, n_used, od_moe_w_gate[0].astype(BF16), od_moe_w_up[0].astype(BF16),
             od_moe_w_down[0].astype(BF16), tm=MOE_ROWS)
    out = _combine(y, slot0, slot1, h, gates, vec(od_ln2_g[0]), vec(od_ln2_b[0]), batch, seq, lp)
    return out.reshape(batch, seq, D_MODEL)
```

```python
import functools

import jax
import jax.numpy as jnp
from jax import lax
from jax.experimental import pallas as pl
from jax.experimental.pallas import tpu as pltpu

F32 = jnp.float32
BF16 = jnp.bfloat16

D_MODEL = 2048
DEPTH = 2
N_META = 16
BLOCK = 128
SWA_HEADS = 16
SWA_KV_HEADS = 2
SWA_HEAD_DIM = 64
SWA_GROUP = SWA_HEADS // SWA_KV_HEADS
MLA_HEADS = 8
MLA_Q_RANK = 768
MLA_KV_RANK = 512
MLA_NOPE_DIM = 128
MLA_ROPE_DIM = 64
MLA_V_DIM = 128
MLA_QK_DIM = MLA_NOPE_DIM + MLA_ROPE_DIM
ROPE_THETA = 10000.0
SWA_Q_W = SWA_HEADS * SWA_HEAD_DIM
SWA_KV_W = SWA_KV_HEADS * SWA_HEAD_DIM
CONV_WIDTH = 31
D_FF = 7 * D_MODEL // 2
N_EXPERTS = 8
TOP_K = 2
DN_ALPHA = (2 * DEPTH) ** 0.25
NEG_INF = -1e30
LN_EPS = 1e-5
RMS_EPS = 1e-6

LANE = 128
SUBLANE = 8
VMEM_LIMIT = 56 * 1024 * 1024

ROW_TILE = 512
FRONT = ROW_TILE - N_META
MLA_PAD_W = 256
FF_TILE = 1024
MOE_ROWS = 512
GATHER_ROWS = 256
HALO = 32


def _cparams(sem):
    return pltpu.CompilerParams(dimension_semantics=sem, vmem_limit_bytes=VMEM_LIMIT)


def _layer_norm(x, g, b):
    mu = jnp.mean(x, axis=-1, keepdims=True)
    xc = x - mu
    var = jnp.mean(xc * xc, axis=-1, keepdims=True)
    return xc * lax.rsqrt(var + LN_EPS) * g + b


def _rms_norm(x, g):
    return x * lax.rsqrt(jnp.mean(x * x, axis=-1, keepdims=True) + RMS_EPS) * g


def _sigmoid(x):
    return 1.0 / (1.0 + jnp.exp(-x))


def _dot(a, b):
    return jnp.dot(a, b, preferred_element_type=F32)


def _dot_nt(a, b):
    return lax.dot_general(a, b, (((1,), (1,)), ((), ())), preferred_element_type=F32)


SWA_K_PAD_W = SWA_KV_HEADS * LANE
IN_ROW_W = SWA_K_PAD_W + MLA_Q_RANK + MLA_KV_RANK + 2 * LANE
IN_T_ROWS = SWA_Q_W + SWA_KV_W


def _inproj_kernel(x_ref, w_ref, wt_ref, cs_ref, aqt_ref, ak_ref, avt_ref, cq_ref, ckv_ref, kr_ref):
    xb = x_ref[...].astype(BF16)
    p = _dot(xb, w_ref[...])
    o = SWA_K_PAD_W
    ak_ref[...] = p[:, :o].astype(BF16)
    cq_ref[...] = p[:, o:o + MLA_Q_RANK]
    o += MLA_Q_RANK
    ckv_ref[...] = p[:, o:o + MLA_KV_RANK]
    o += MLA_KV_RANK
    kr = p[:, o:o + LANE]
    kr_rot = p[:, o + LANE:o + 2 * LANE]
    kr_ref[...] = (kr * cs_ref[:, :LANE] + kr_rot * cs_ref[:, LANE:]).astype(BF16)
    pt = _dot_nt(wt_ref[...], xb)
    aqt_ref[...] = pt[:SWA_Q_W, :].astype(BF16)
    avt_ref[...] = pt[SWA_Q_W:, :].astype(BF16)


def _inproj(h, w_row, w_t, cs, tm=256):
    tp = h.shape[0]
    row = lambda w: pl.BlockSpec((tm, w), lambda i: (i, 0))
    col = lambda r: pl.BlockSpec((r, tm), lambda i: (0, i))
    full = lambda a, b: pl.BlockSpec((a, b), lambda i: (0, 0))
    return pl.pallas_call(
        _inproj_kernel,
        grid=(tp // tm,),
        in_specs=[row(D_MODEL), full(D_MODEL, IN_ROW_W), full(IN_T_ROWS, D_MODEL), row(2 * LANE)],
        out_specs=[col(SWA_Q_W), row(SWA_K_PAD_W), col(SWA_KV_W),
                   row(MLA_Q_RANK), row(MLA_KV_RANK), row(LANE)],
        out_shape=[jax.ShapeDtypeStruct((SWA_Q_W, tp), BF16),
                   jax.ShapeDtypeStruct((tp, SWA_K_PAD_W), BF16),
                   jax.ShapeDtypeStruct((SWA_KV_W, tp), BF16),
                   jax.ShapeDtypeStruct((tp, MLA_Q_RANK), F32),
                   jax.ShapeDtypeStruct((tp, MLA_KV_RANK), F32),
                   jax.ShapeDtypeStruct((tp, LANE), BF16)],
        compiler_params=_cparams(("parallel",)),
        name="inproj",
    )(h, w_row, w_t, cs)


MLA_QT_ROWS = MLA_HEADS * MLA_PAD_W + MLA_HEADS * MLA_ROPE_DIM
LOG2E = 1.4426950408889634


def _mla_up_kernel(cq_ref, ckv_ref, kr_ref, cst_ref, gq_ref, gkv_ref, wqt_ref, wk_ref, wvt_ref,
                   qt_ref, kk_ref, vt_ref):
    scale = MLA_QK_DIM ** -0.5 * LOG2E
    tm = cq_ref.shape[0]
    cos = cst_ref[:MLA_ROPE_DIM, :]
    sin = cst_ref[MLA_ROPE_DIM:, :]
    cqn = _rms_norm(cq_ref[...], gq_ref[...]).astype(BF16)
    qa = _dot_nt(wqt_ref[...], cqn)
    rot0 = MLA_HEADS * MLA_PAD_W
    for h in range(MLA_HEADS):
        a = h * MLA_PAD_W
        r = a + MLA_NOPE_DIM
        qt_ref[a:r, :] = (qa[a:r, :] * scale).astype(BF16)
        roped = (qa[r:r + MLA_ROPE_DIM, :] * cos
                 + qa[rot0 + h * MLA_ROPE_DIM:rot0 + (h + 1) * MLA_ROPE_DIM, :] * sin)
        qt_ref[r:r + MLA_ROPE_DIM, :] = (roped * scale).astype(BF16)
        qt_ref[r + MLA_ROPE_DIM:a + MLA_PAD_W, :] = jnp.zeros((MLA_PAD_W - MLA_QK_DIM, tm), BF16)
    ckvn = _rms_norm(ckv_ref[...], gkv_ref[...]).astype(BF16)
    kn = _dot(ckvn, wk_ref[...])
    kr = kr_ref[...]
    for h in range(MLA_HEADS):
        kk_ref[:, h * MLA_PAD_W:h * MLA_PAD_W + LANE] = kn[:, h * MLA_NOPE_DIM:(h + 1) * MLA_NOPE_DIM].astype(BF16)
        kk_ref[:, h * MLA_PAD_W + LANE:(h + 1) * MLA_PAD_W] = kr
    vt_ref[0] = _dot_nt(wvt_ref[...], ckvn).astype(BF16)


def _mla_up(cq, ckv, kr, cst, gq, gkv, wqt, wk, wvt, tm=ROW_TILE):
    tp = cq.shape[0]
    row = lambda w: pl.BlockSpec((tm, w), lambda i: (i, 0))
    col = lambda r: pl.BlockSpec((r, tm), lambda i: (0, i))
    full = lambda a, b: pl.BlockSpec((a, b), lambda i: (0, 0))
    kw = MLA_HEADS * MLA_PAD_W
    vw = MLA_HEADS * MLA_V_DIM
    return pl.pallas_call(
        _mla_up_kernel,
        grid=(tp // tm,),
        in_specs=[row(MLA_Q_RANK), row(MLA_KV_RANK), row(LANE), col(2 * MLA_ROPE_DIM),
                  full(1, MLA_Q_RANK), full(1, MLA_KV_RANK),
                  full(MLA_QT_ROWS, MLA_Q_RANK), full(MLA_KV_RANK, MLA_HEADS * MLA_NOPE_DIM),
                  full(vw, MLA_KV_RANK)],
        out_specs=[col(kw), row(kw), pl.BlockSpec((1, vw, tm), lambda i: (i, 0, 0))],
        out_shape=[jax.ShapeDtypeStruct((kw, tp), BF16),
                   jax.ShapeDtypeStruct((tp, kw), BF16),
                   jax.ShapeDtypeStruct((tp // tm, vw, tm), BF16)],
        compiler_params=_cparams(("parallel",)),
        name="mla_up",
    )(cq, ckv, kr, cst, gq, gkv, wqt, wk, wvt)


META_BLOCK = FRONT // BLOCK
META_ROW0 = FRONT % BLOCK


def _swa_kernel(sink_ref, qt_ref, kc_ref, kp_ref, km_ref, vc_ref, vp_ref, vm_ref, o_ref):
    n = pl.program_id(1) - META_BLOCK
    r = lax.broadcasted_iota(jnp.int32, (3 * BLOCK, BLOCK), 0)
    i = lax.broadcasted_iota(jnp.int32, (3 * BLOCK, BLOCK), 1)
    never = 4 * BLOCK
    ok_meta = (r >= META_ROW0) & (r <= n * BLOCK + i)
    ok_prev = (r - BLOCK) > (i + jnp.where(n >= 2, 0, never))
    ok_cur = (r - 2 * BLOCK + jnp.where(n >= 1, 0, never)) <= i
    ok = ((r < BLOCK) & ok_meta) | ((r >= BLOCK) & (r < 2 * BLOCK) & ok_prev) | ((r >= 2 * BLOCK) & ok_cur)
    bias = jnp.where(ok, 0.0, NEG_INF)
    bias = jnp.concatenate([bias] * SWA_GROUP, axis=1)
    km, kp, kc = km_ref[...], kp_ref[...], kc_ref[...]
    outs = []
    for g in range(SWA_KV_HEADS):
        ks = slice(g * LANE, g * LANE + SWA_HEAD_DIM)
        ds = slice(g * SWA_HEAD_DIM, (g + 1) * SWA_HEAD_DIM)
        k = jnp.concatenate([km[:, ks], kp[:, ks], kc[:, ks]], axis=0)
        vt = jnp.concatenate([vm_ref[ds, :], vp_ref[ds, :], vc_ref[ds, :]], axis=1)
        qt = jnp.concatenate([qt_ref[(g * SWA_GROUP + hh) * SWA_HEAD_DIM:(g * SWA_GROUP + hh + 1) * SWA_HEAD_DIM, :]
                              for hh in range(SWA_GROUP)], axis=1)
        s = _dot(k, qt) + bias
        sink = sink_ref[g:g + 1, :]
        m = jnp.maximum(jnp.max(s, axis=0, keepdims=True), sink)
        p = jnp.exp2(s - m)
        den = jnp.sum(p, axis=0, keepdims=True) + jnp.exp2(sink - m)
        o = _dot(vt, p.astype(BF16)) / den
        outs += [o[:, hh * BLOCK:(hh + 1) * BLOCK] for hh in range(SWA_GROUP)]
    o_ref[0] = jnp.concatenate(outs, axis=0).T.astype(BF16)


def _swa(aqt, ak, avt, sink_rows, batch):
    tp = ak.shape[0]
    nb = tp // batch // BLOCK
    cur = lambda bi, n: bi * nb + n
    prev = lambda bi, n: bi * nb + jnp.maximum(n - 1, 0)
    meta = lambda bi, n: bi * nb + META_BLOCK
    kspec = lambda f: pl.BlockSpec((BLOCK, SWA_K_PAD_W), lambda bi, n: (f(bi, n), 0))
    vspec = lambda f: pl.BlockSpec((SWA_KV_W, BLOCK), lambda bi, n: (0, f(bi, n)))
    return pl.pallas_call(
        _swa_kernel,
        grid=(batch, nb),
        in_specs=[pl.BlockSpec((SWA_KV_HEADS, SWA_GROUP * BLOCK), lambda bi, n: (0, 0)),
                  pl.BlockSpec((SWA_Q_W, BLOCK), lambda bi, n: (0, cur(bi, n))),
                  kspec(cur), kspec(prev), kspec(meta), vspec(cur), vspec(prev), vspec(meta)],
        out_specs=pl.BlockSpec((1, BLOCK, SWA_Q_W), lambda bi, n: (bi, n, 0)),
        out_shape=jax.ShapeDtypeStruct((batch, nb * BLOCK, SWA_Q_W), BF16),
        compiler_params=_cparams(("parallel", "parallel")),
        name="swa",
    )(sink_rows, aqt, ak, ak, ak, avt, avt, avt)


def _mla_kernel(qt_ref, k_ref, vt_ref, o_ref, sa_scr, sb_scr, *, tq):
    qi = pl.program_id(2)
    qt = qt_ref[...]
    meta0 = FRONT // LANE * LANE

    def scores(ki):
        start = pl.multiple_of(ki * tq, tq)
        return _dot(k_ref[0, pl.ds(start, tq), :], qt)

    def masked(s, row0):
        kidx = row0 + lax.broadcasted_iota(jnp.int32, s.shape, 0)
        qidx = qi * tq + lax.broadcasted_iota(jnp.int32, s.shape, 1)
        return jnp.where((kidx <= qidx) & (kidx >= FRONT), s, NEG_INF)

    def update(s, smax, vb, carry):
        m, l, acc = carry
        m_new = jnp.maximum(m, smax)
        alpha = jnp.exp2(m - m_new)
        p = jnp.exp2(s - m_new)
        l = alpha * l + jnp.sum(p, axis=0, keepdims=True)
        acc = alpha * acc + _dot(vb, p.astype(BF16))
        return m_new, l, acc

    def update_masked(s, row0, vb, carry):
        s = masked(s, row0)
        return update(s, jnp.max(s, axis=0, keepdims=True), vb, carry)

    init = (jnp.full((1, tq), NEG_INF, F32), jnp.zeros((1, tq), F32), jnp.zeros((MLA_V_DIM, tq), F32))
    carry = update_masked(_dot(k_ref[0, meta0:tq, :], qt), meta0, vt_ref[0, :, meta0:tq], init)

    def issue(ki, dst):
        s = scores(ki)
        dst[...] = s
        return jnp.max(s, axis=0, keepdims=True)

    def consume(src, smax, ki, carry):
        return update(src[...], smax, vt_ref[ki], carry)

    def diagonal(src, carry):
        return update_masked(src[...], qi * tq, vt_ref[qi], carry)

    def tiles_after_first(carry):
        n_full = qi - 1

        def pair(t, c):
            smax_a, carry = c
            ki = 1 + 2 * t
            smax_b = issue(ki + 1, sb_scr)
            carry = consume(sa_scr, smax_a, ki, carry)
            smax_a = issue(ki + 2, sa_scr)
            carry = consume(sb_scr, smax_b, ki + 1, carry)
            return smax_a, carry

        smax_a, carry = lax.fori_loop(0, n_full // 2, pair, (issue(1, sa_scr), carry))

        def odd_tail(carry):
            issue(qi, sb_scr)
            return diagonal(sb_scr, consume(sa_scr, smax_a, qi - 1, carry))

        return lax.cond(n_full % 2 == 1, odd_tail, lambda c: diagonal(sa_scr, c), carry)

    _, l, acc = lax.cond(qi > 0, tiles_after_first, lambda c: c, carry)
    o_ref[0] = (acc / l).T.astype(BF16)


def _mla(qt, kk, vt, batch, tq=ROW_TILE):
    lp = kk.shape[1]
    nq = lp // tq
    return pl.pallas_call(
        functools.partial(_mla_kernel, tq=tq),
        grid=(batch, MLA_HEADS, nq),
        in_specs=[pl.BlockSpec((MLA_PAD_W, tq), lambda bi, h, i: (h, bi * nq + i)),
                  pl.BlockSpec((1, lp, MLA_PAD_W), lambda bi, h, i: (bi, 0, h)),
                  pl.BlockSpec((nq, MLA_V_DIM, tq), lambda bi, h, i: (bi, h, 0))],
        out_specs=pl.BlockSpec((1, tq, MLA_V_DIM), lambda bi, h, i: (bi, i, h)),
        out_shape=jax.ShapeDtypeStruct((batch, lp, MLA_HEADS * MLA_V_DIM), BF16),
        scratch_shapes=[pltpu.VMEM((tq, tq), F32), pltpu.VMEM((tq, tq), F32)],
        compiler_params=_cparams(("parallel", "parallel", "arbitrary")),
        name="mla",
    )(qt, kk, vt)


def _outproj_kernel(oa_ref, ob_ref, h_ref, w_ref, g_ref, b_ref, o_ref):
    mix = _dot(oa_ref[...], w_ref[:SWA_Q_W, :]) + _dot(ob_ref[...], w_ref[SWA_Q_W:, :])
    o_ref[...] = _layer_norm(DN_ALPHA * h_ref[...] + mix, g_ref[...], b_ref[...])


def _outproj(oa, ob, h, w_o, g, b, tm=ROW_TILE):
    tp = h.shape[0]
    row = lambda w: pl.BlockSpec((tm, w), lambda i: (i, 0))
    full = lambda a, c: pl.BlockSpec((a, c), lambda i: (0, 0))
    return pl.pallas_call(
        _outproj_kernel,
        grid=(tp // tm,),
        in_specs=[row(SWA_Q_W), row(MLA_HEADS * MLA_V_DIM), row(D_MODEL),
                  full(D_MODEL, D_MODEL), full(1, D_MODEL), full(1, D_MODEL)],
        out_specs=row(D_MODEL),
        out_shape=jax.ShapeDtypeStruct((tp, D_MODEL), F32),
        compiler_params=_cparams(("parallel",)),
        name="outproj_ln",
    )(oa, ob, h, w_o, g, b)


def _swiglu_steps(x_bf16, wg_ref, wu_ref, wd_ref, act_scr, acc_scr):
    j = pl.program_id(1)
    last = pl.num_programs(1) - 1

    def act_tile():
        xb = x_bf16()
        a = _dot(xb, wg_ref[0])
        u = _dot(xb, wu_ref[0])
        return (a * _sigmoid(a) * u).astype(BF16)

    @pl.when(j == 0)
    def _():
        acc_scr[...] = jnp.zeros_like(acc_scr)
        act_scr[...] = act_tile()

    @pl.when((j > 0) & (j < last))
    def _():
        acc_scr[...] += _dot(act_scr[...], wd_ref[0])
        act_scr[...] = act_tile()

    @pl.when(j == last)
    def _():
        acc_scr[...] += _dot(act_scr[...], wd_ref[0])


def _ffn_dense_kernel(be_ref, nu_ref, x_ref, wg_ref, wu_ref, wd_ref, g_ref, b_ref, o_ref,
                      xb_scr, act_scr, acc_scr):
    @pl.when(pl.program_id(1) == 0)
    def _():
        xb_scr[...] = x_ref[...].astype(BF16)

    _swiglu_steps(lambda: xb_scr[...], wg_ref, wu_ref, wd_ref, act_scr, acc_scr)

    @pl.when(pl.program_id(1) == pl.num_programs(1) - 1)
    def _():
        o_ref[...] = _layer_norm(DN_ALPHA * x_ref[...] + acc_scr[...], g_ref[...], b_ref[...])


def _ffn_routed_kernel(be_ref, nu_ref, x_ref, wg_ref, wu_ref, wd_ref, o_ref, act_scr, acc_scr):
    j = pl.program_id(1)

    @pl.when(pl.program_id(0) < nu_ref[0])
    def _():
        _swiglu_steps(lambda: x_ref[...], wg_ref, wu_ref, wd_ref, act_scr, acc_scr)

        @pl.when(j == pl.num_programs(1) - 1)
        def _():
            o_ref[...] = acc_scr[...]

    @pl.when((pl.program_id(0) >= nu_ref[0]) & (j == pl.num_programs(1) - 1))
    def _():
        o_ref[...] = jnp.zeros_like(o_ref)


def _ffn(x, block_expert, n_used, w_gate, w_up, w_down, ln=None, tm=ROW_TILE, tf=FF_TILE):
    rows = x.shape[0]
    nj = D_FF // tf
    blk = lambda i, nu: jnp.minimum(i, nu[0] - 1)
    ff_in = lambda i, j, nu: jnp.where(i < nu[0], jnp.minimum(j, nj - 1), nj - 1)
    ff_out = lambda i, j, nu: jnp.where(i < nu[0], jnp.maximum(j - 1, 0), nj - 1)
    row = pl.BlockSpec((tm, D_MODEL), lambda i, j, be, nu: (blk(i, nu), 0))
    in_specs = [row,
                pl.BlockSpec((1, D_MODEL, tf), lambda i, j, be, nu: (be[blk(i, nu)], 0, ff_in(i, j, nu))),
                pl.BlockSpec((1, D_MODEL, tf), lambda i, j, be, nu: (be[blk(i, nu)], 0, ff_in(i, j, nu))),
                pl.BlockSpec((1, tf, D_MODEL), lambda i, j, be, nu: (be[blk(i, nu)], ff_out(i, j, nu), 0))]
    scratch = [pltpu.VMEM((tm, tf), BF16), pltpu.VMEM((tm, D_MODEL), F32)]
    args = [block_expert, n_used, x, w_gate, w_up, w_down]
    if ln is not None:
        vec = pl.BlockSpec((1, D_MODEL), lambda i, j, be, nu: (0, 0))
        in_specs += [vec, vec]
        scratch = [pltpu.VMEM((tm, D_MODEL), BF16)] + scratch
        args += list(ln)
        body, name = _ffn_dense_kernel, "ffn_dense_ln"
    else:
        body, name = _ffn_routed_kernel, "ffn_routed"
    return pl.pallas_call(
        body,
        grid_spec=pltpu.PrefetchScalarGridSpec(
            num_scalar_prefetch=2,
            grid=(rows // tm, nj + 1),
            in_specs=in_specs,
            out_specs=pl.BlockSpec((tm, D_MODEL), lambda i, j, be, nu: (i, 0)),
            scratch_shapes=scratch),
        out_shape=jax.ShapeDtypeStruct((rows, D_MODEL), F32),
        compiler_params=_cparams(("arbitrary", "arbitrary")),
        name=name,
    )(*args)


def _pw1_kernel(h_ref, wa_ref, wg_ref, ba_ref, bg_ref, u_ref, xb_scr):
    @pl.when(pl.program_id(1) == 0)
    def _():
        xb_scr[...] = h_ref[...].astype(BF16)

    xb = xb_scr[...]
    a = _dot(xb, wa_ref[...]) + ba_ref[...]
    g = _dot(xb, wg_ref[...]) + bg_ref[...]
    u_ref[...] = a * _sigmoid(g)


def _pw1(h, w_pw1, b_pw1, tm=ROW_TILE, tn=1024):
    tp = h.shape[0]
    nj = D_MODEL // tn
    return pl.pallas_call(
        _pw1_kernel,
        grid=(tp // tm, nj),
        in_specs=[pl.BlockSpec((tm, D_MODEL), lambda i, j: (i, 0)),
                  pl.BlockSpec((D_MODEL, tn), lambda i, j: (0, j)),
                  pl.BlockSpec((D_MODEL, tn), lambda i, j: (0, j + nj)),
                  pl.BlockSpec((1, tn), lambda i, j: (0, j)),
                  pl.BlockSpec((1, tn), lambda i, j: (0, j + nj))],
        out_specs=pl.BlockSpec((tm, tn), lambda i, j: (i, j)),
        out_shape=jax.ShapeDtypeStruct((tp, D_MODEL), F32),
        scratch_shapes=[pltpu.VMEM((tm, D_MODEL), BF16)],
        compiler_params=_cparams(("parallel", "arbitrary")),
        name="pw1_glu",
    )(h, w_pw1, w_pw1, b_pw1, b_pw1)


CONV_ROWS = 32
CONV_COLS = 256


def _conv_kernel(uc_ref, up_ref, h_ref, wdw_ref, bdw_ref, gcn_ref, bcn_ref, w2_ref, b2_ref,
                 g1_ref, b1_ref, wr_ref, br_ref, ho_ref, idx_ref, gate_ref, win_scr, cv_scr,
                 *, tm, tiles_per_batch):
    row0 = (pl.program_id(0) % tiles_per_batch) * tm
    win_scr[HALO:, :] = uc_ref[...]
    win_scr[:HALO, :] = up_ref[...]

    @pl.when(row0 < FRONT)
    def _():
        r = lax.broadcasted_iota(jnp.int32, (tm + HALO, 1), 0) + (row0 - HALO)
        win_scr[...] = jnp.where(r >= FRONT, win_scr[...], 0.0)

    off = HALO - (CONV_WIDTH - 1)

    def rows_body(rc, _):
        r0 = pl.multiple_of(rc * CONV_ROWS, CONV_ROWS)
        for cc in range(D_MODEL // CONV_COLS):
            cs = slice(cc * CONV_COLS, (cc + 1) * CONV_COLS)
            blk = win_scr[pl.ds(r0, CONV_ROWS + HALO), cs]
            acc = jnp.zeros((CONV_ROWS, CONV_COLS), F32)
            for b in range(SUBLANE):
                ub = pltpu.roll(blk, CONV_ROWS + HALO - (off + b), axis=0)
                for a in range((CONV_WIDTH - 1 - b) // SUBLANE + 1):
                    k = SUBLANE * a + b
                    acc = acc + ub[SUBLANE * a:SUBLANE * a + CONV_ROWS, :] * wdw_ref[k:k + 1, cs]
            cv_scr[pl.ds(r0, CONV_ROWS), cs] = acc
        return 0

    lax.fori_loop(0, tm // CONV_ROWS, rows_body, 0)

    c = _layer_norm(cv_scr[...] + bdw_ref[...], gcn_ref[...], bcn_ref[...])
    c = c * _sigmoid(c)
    mix = _dot(c.astype(BF16), w2_ref[...]) + b2_ref[...]
    hn = _layer_norm(DN_ALPHA * h_ref[...] + mix, g1_ref[...], b1_ref[...])
    ho_ref[...] = hn

    lg = [jnp.sum(hn * wr_ref[e:e + 1, :], axis=-1, keepdims=True) + br_ref[e] for e in range(N_EXPERTS)]

    def top1(vals):
        best = vals[0]
        for v in vals[1:]:
            best = jnp.maximum(best, v)
        idx = jnp.full(best.shape, N_EXPERTS - 1, jnp.int32)
        for e in range(N_EXPERTS - 2, -1, -1):
            idx = jnp.where(vals[e] == best, e, idx)
        return best, idx

    m1, i1 = top1(lg)
    m2, i2 = top1([jnp.where(i1 == e, -jnp.inf, lg[e]) for e in range(N_EXPERTS)])
    lane = lax.broadcasted_iota(jnp.int32, (hn.shape[0], LANE), 1)
    e = jnp.exp(m2 - m1)
    gate1 = 1.0 / (1.0 + e)
    gate2 = e * gate1
    idx_ref[...] = jnp.where(lane == 0, i1, jnp.where(lane == 1, i2, 0))
    gate_ref[...] = jnp.where(lane == 0, gate1, jnp.where(lane == 1, gate2, 0.0))


def _conv_module(u, h, wdw, bdw, gcn, bcn, w2, b2, g1, b1, wr, br, tiles_per_batch, tm=256):
    tp = h.shape[0]
    row = lambda w: pl.BlockSpec((tm, w), lambda i: (i, 0))
    full = lambda a, c: pl.BlockSpec((a, c), lambda i: (0, 0))
    halo_blocks = tm // HALO
    return pl.pallas_call(
        functools.partial(_conv_kernel, tm=tm, tiles_per_batch=tiles_per_batch * (ROW_TILE // tm)),
        grid=(tp // tm,),
        in_specs=[row(D_MODEL),
                  pl.BlockSpec((HALO, D_MODEL), lambda i: (jnp.maximum(i * halo_blocks - 1, 0), 0)),
                  row(D_MODEL),
                  full(HALO, D_MODEL), full(1, D_MODEL), full(1, D_MODEL), full(1, D_MODEL),
                  full(D_MODEL, D_MODEL), full(1, D_MODEL), full(1, D_MODEL), full(1, D_MODEL),
                  full(N_EXPERTS, D_MODEL), pl.BlockSpec(memory_space=pltpu.SMEM)],
        out_specs=[row(D_MODEL), row(LANE), row(LANE)],
        out_shape=[jax.ShapeDtypeStruct((tp, D_MODEL), F32),
                   jax.ShapeDtypeStruct((tp, LANE), jnp.int32),
                   jax.ShapeDtypeStruct((tp, LANE), F32)],
        scratch_shapes=[pltpu.VMEM((tm + HALO, D_MODEL), F32), pltpu.VMEM((tm, D_MODEL), F32)],
        compiler_params=_cparams(("parallel",)),
        name="conv_module_ln_router",
    )(u, u, h, wdw, bdw, gcn, bcn, w2, b2, g1, b1, wr, br)


def _row_copy(src_hbm, row, buf, slot, sem):
    return pltpu.make_async_copy(src_hbm.at[pl.ds(row, 1), :], buf.at[pl.ds(slot, 1), :], sem)


def _issue_rows(src_hbm, idx_ref, buf, sem, rows):
    def body(r, _):
        _row_copy(src_hbm, idx_ref[0, 0, r], buf, r, sem).start()
        return 0

    lax.fori_loop(0, rows, body, 0, unroll=8)


def _drain_rows(src_hbm, buf, sem, rows):
    def body(r, _):
        _row_copy(src_hbm, 0, buf, r, sem).wait()
        return 0

    lax.fori_loop(0, rows, body, 0, unroll=8)


def _gather_kernel(idx_ref, nxt_ref, src_hbm, o_ref, buf, sem, *, rows):
    i = pl.program_id(0)
    slot = i % 2

    @pl.when(i == 0)
    def _():
        _issue_rows(src_hbm, idx_ref, buf.at[0], sem.at[0], rows)

    @pl.when(i + 1 < pl.num_programs(0))
    def _():
        _issue_rows(src_hbm, nxt_ref, buf.at[1 - slot], sem.at[1 - slot], rows)

    _drain_rows(src_hbm, buf.at[slot], sem.at[slot], rows)
    o_ref[...] = buf[slot].astype(o_ref.dtype)


def _gather_rows(src, idx, rows=GATHER_ROWS):
    n = idx.shape[0]
    steps = n // rows
    islot = lambda f: pl.BlockSpec((1, 1, rows), lambda i: (f(i), 0, 0), memory_space=pltpu.SMEM)
    idx3 = idx.reshape(steps, 1, rows)
    return pl.pallas_call(
        functools.partial(_gather_kernel, rows=rows),
        grid=(steps,),
        in_specs=[islot(lambda i: i), islot(lambda i: jnp.minimum(i + 1, steps - 1)),
                  pl.BlockSpec(memory_space=pl.ANY)],
        out_specs=pl.BlockSpec((rows, D_MODEL), lambda i: (i, 0)),
        out_shape=jax.ShapeDtypeStruct((n, D_MODEL), BF16),
        scratch_shapes=[pltpu.VMEM((2, rows, D_MODEL), F32), pltpu.SemaphoreType.DMA((2,))],
        compiler_params=_cparams(("arbitrary",)),
        name="dispatch_gather",
    )(idx3, idx3, src)


def _combine_kernel(i0_ref, i1_ref, n0_ref, n1_ref, y_hbm, h_ref, gate_ref, g_ref, b_ref, o_ref,
                    buf0, buf1, sem, *, rows):
    t = pl.program_id(0) * pl.num_programs(1) + pl.program_id(1)
    steps = pl.num_programs(0) * pl.num_programs(1)
    slot = t % 2

    def issue(a_ref, b_ref, s):
        _issue_rows(y_hbm, a_ref, buf0.at[s], sem.at[s], rows)
        _issue_rows(y_hbm, b_ref, buf1.at[s], sem.at[s], rows)

    @pl.when(t == 0)
    def _():
        issue(i0_ref, i1_ref, 0)

    @pl.when(t + 1 < steps)
    def _():
        issue(n0_ref, n1_ref, 1 - slot)

    _drain_rows(y_hbm, buf0.at[slot], sem.at[slot], rows)
    _drain_rows(y_hbm, buf1.at[slot], sem.at[slot], rows)
    gate = gate_ref[...]
    y = buf0[slot] * gate[:, 0:1] + buf1[slot] * gate[:, 1:2]
    o_ref[...] = _layer_norm(DN_ALPHA * h_ref[...] + y, g_ref[...], b_ref[...])


def _combine(y, slot0, slot1, h, gate, g, b, batch, seq, lp, rows=GATHER_ROWS):
    per_batch = seq // rows
    lead = ROW_TILE // rows
    lp_blocks = lp // rows
    steps = batch * per_batch
    hrow = lambda w: pl.BlockSpec((rows, w), lambda bi, j: (bi * lp_blocks + lead + j, 0))
    cur = lambda bi, j: bi * per_batch + j
    nxt = lambda bi, j: jnp.minimum(bi * per_batch + j + 1, steps - 1)
    islot = lambda f: pl.BlockSpec((1, 1, rows), lambda bi, j: (f(bi, j), 0, 0), memory_space=pltpu.SMEM)
    vec = pl.BlockSpec((1, D_MODEL), lambda bi, j: (0, 0))
    s0 = slot0.reshape(steps, 1, rows)
    s1 = slot1.reshape(steps, 1, rows)
    return pl.pallas_call(
        functools.partial(_combine_kernel, rows=rows),
        grid=(batch, per_batch),
        in_specs=[islot(cur), islot(cur), islot(nxt), islot(nxt),
                  pl.BlockSpec(memory_space=pl.ANY), hrow(D_MODEL), hrow(LANE), vec, vec],
        out_specs=pl.BlockSpec((rows, D_MODEL), lambda bi, j: (cur(bi, j), 0)),
        out_shape=jax.ShapeDtypeStruct((batch * seq, D_MODEL), F32),
        scratch_shapes=[pltpu.VMEM((2, rows, D_MODEL), F32), pltpu.VMEM((2, rows, D_MODEL), F32),
                        pltpu.SemaphoreType.DMA((2,))],
        compiler_params=_cparams(("arbitrary", "arbitrary")),
        name="combine_ln",
    )(s0, s1, s0, s1, y, h, gate, g, b)


def _rot_cols(w):
    half = MLA_ROPE_DIM // 2
    return jnp.concatenate([-w[..., half:], w[..., :half]], axis=-1)


def _rope_table(batch, lp):
    half = MLA_ROPE_DIM // 2
    inv = ROPE_THETA ** (-jnp.arange(half, dtype=F32) * 2.0 / MLA_ROPE_DIM)
    pos = (jnp.arange(lp) - FRONT).astype(F32)
    ang = pos[:, None] * inv[None, :]
    z = jnp.zeros((lp, LANE - MLA_ROPE_DIM), F32)
    cos = jnp.concatenate([jnp.cos(ang), jnp.cos(ang), z], axis=-1)
    sin = jnp.concatenate([jnp.sin(ang), jnp.sin(ang), z], axis=-1)
    return jnp.tile(jnp.concatenate([cos, sin], axis=-1), (batch, 1))


def _pad_cols(w, width):
    return jnp.pad(w, ((0, 0), (0, width - w.shape[-1])))


def _routing(top_idx, batch, seq, lp):
    t = batch * seq
    a = t * TOP_K
    real = top_idx.reshape(batch, lp, LANE)[:, ROW_TILE:, :TOP_K]
    flat_e = real.reshape(a)
    tok_row = (jnp.arange(batch)[:, None] * lp + ROW_TILE + jnp.arange(seq)[None, :]).reshape(t)
    flat_row = jnp.repeat(tok_row, TOP_K).astype(jnp.int32)
    onehot = (flat_e[:, None] == jnp.arange(N_EXPERTS)[None, :]).astype(jnp.int32)
    csum = jnp.cumsum(onehot, axis=0)
    rank = jnp.take_along_axis(csum, flat_e[:, None], axis=1)[:, 0] - 1
    counts = csum[-1]
    padded = (counts + MOE_ROWS - 1) // MOE_ROWS * MOE_ROWS
    pend = jnp.cumsum(padded)
    pstart = pend - padded
    dest = (pstart[flat_e] + rank).astype(jnp.int32)
    nblk = a // MOE_ROWS + N_EXPERTS
    p = nblk * MOE_ROWS
    src = jnp.zeros((p,), jnp.int32).at[dest].set(flat_row)
    blk_e = jnp.minimum(jnp.sum(jnp.arange(nblk)[:, None] * MOE_ROWS >= pend[None, :], -1),
                        N_EXPERTS - 1).astype(jnp.int32)
    n_used = (pend[-1:] // MOE_ROWS).astype(jnp.int32)
    dest2 = dest.reshape(t, TOP_K)
    return src, blk_e, n_used, dest2[:, 0], dest2[:, 1]


def kernel(x, meta_tokens, ev_w_in, ev_sinks, ev_g_cq, ev_w_uq, ev_g_ckv, ev_w_ukv, ev_w_o, ev_ln1_g, ev_ln1_b, ev_ffn_w_gate, ev_ffn_w_up, ev_ffn_w_down, ev_ln2_g, ev_ln2_b, od_w_pw1, od_b_pw1, od_w_dw, od_b_dw, od_g_cn, od_b_cn, od_w_pw2, od_b_pw2, od_ln1_g, od_ln1_b, od_w_router, od_b_router, od_moe_w_gate, od_moe_w_up, od_moe_w_down, od_ln2_g, od_ln2_b):
    batch, seq, _ = x.shape
    assert seq % ROW_TILE == 0
    lp = ROW_TILE + seq
    tp = batch * lp
    vec = lambda v: v.reshape(1, -1).astype(F32)

    meta = jnp.broadcast_to(meta_tokens[None].astype(x.dtype), (batch, N_META, D_MODEL))
    h = jnp.concatenate([jnp.zeros((batch, FRONT, D_MODEL), x.dtype), meta, x], axis=1).reshape(tp, D_MODEL)
    cs = _rope_table(batch, lp)

    w_in = ev_w_in[0]
    o = SWA_Q_W + 2 * SWA_KV_W + MLA_Q_RANK + MLA_KV_RANK
    w_kr = w_in[:, o:]
    w_ak = w_in[:, SWA_Q_W:SWA_Q_W + SWA_KV_W].reshape(D_MODEL, SWA_KV_HEADS, SWA_HEAD_DIM)
    w_ak = jnp.pad(w_ak, ((0, 0), (0, 0), (0, LANE - SWA_HEAD_DIM))).reshape(D_MODEL, SWA_K_PAD_W)
    w_row = jnp.concatenate([w_ak, w_in[:, SWA_Q_W + 2 * SWA_KV_W:o],
                             _pad_cols(w_kr, LANE), _pad_cols(_rot_cols(w_kr), LANE)], axis=1).astype(BF16)
    w_t = jnp.concatenate([w_in[:, :SWA_Q_W] * (SWA_HEAD_DIM ** -0.5 * LOG2E),
                           w_in[:, SWA_Q_W + SWA_KV_W:SWA_Q_W + 2 * SWA_KV_W]], axis=1).T.astype(BF16)
    aqt, ak, avt, cq, ckv, kr = _inproj(h, w_row, w_t, cs)

    w_uq = ev_w_uq[0].reshape(MLA_Q_RANK, MLA_HEADS, MLA_QK_DIM)
    w_uq_pad = jnp.pad(w_uq, ((0, 0), (0, 0), (0, MLA_PAD_W - MLA_QK_DIM))).reshape(MLA_Q_RANK, -1)
    w_uq_rot = _rot_cols(w_uq[..., MLA_NOPE_DIM:]).reshape(MLA_Q_RANK, -1)
    wqt = jnp.concatenate([w_uq_pad, w_uq_rot], axis=1).T.astype(BF16)
    w_ukv = ev_w_ukv[0].reshape(MLA_KV_RANK, MLA_HEADS, MLA_NOPE_DIM + MLA_V_DIM)
    wk = w_ukv[..., :MLA_NOPE_DIM].reshape(MLA_KV_RANK, -1).astype(BF16)
    wvt = w_ukv[..., MLA_NOPE_DIM:].reshape(MLA_KV_RANK, -1).T.astype(BF16)
    cst = jnp.concatenate([cs[:, :MLA_ROPE_DIM], cs[:, LANE:LANE + MLA_ROPE_DIM]], axis=1).T
    qt, kk, vt = _mla_up(cq, ckv, kr, cst, vec(ev_g_cq[0]), vec(ev_g_ckv[0]), wqt, wk, wvt)

    sink_rows = jnp.repeat(ev_sinks[0].astype(F32) * LOG2E, BLOCK).reshape(SWA_KV_HEADS, SWA_GROUP * BLOCK)
    out_a = _swa(aqt, ak, avt, sink_rows, batch)
    out_b = _mla(qt, kk.reshape(batch, lp, -1), vt, batch)
    h = _outproj(out_a.reshape(tp, -1), out_b.reshape(tp, -1), h, ev_w_o[0].astype(BF16),
                 vec(ev_ln1_g[0]), vec(ev_ln1_b[0]))

    n_dense = tp // ROW_TILE
    h = _ffn(h, jnp.zeros((n_dense,), jnp.int32), jnp.full((1,), n_dense, jnp.int32),
             ev_ffn_w_gate.astype(BF16), ev_ffn_w_up.astype(BF16),
             ev_ffn_w_down.astype(BF16), ln=(vec(ev_ln2_g[0]), vec(ev_ln2_b[0])))

    u = _pw1(h, od_w_pw1[0].astype(BF16), vec(od_b_pw1[0]))
    wdw = jnp.pad(od_w_dw[0].astype(F32), ((0, HALO - CONV_WIDTH), (0, 0)))
    h, top_idx, gates = _conv_module(
        u, h, wdw, vec(od_b_dw[0]), vec(od_g_cn[0]), vec(od_b_cn[0]),
        od_w_pw2[0].astype(BF16), vec(od_b_pw2[0]), vec(od_ln1_g[0]), vec(od_ln1_b[0]),
        od_w_router[0].astype(F32).T, od_b_router[0].astype(F32),
        tiles_per_batch=lp // ROW_TILE)

    src, blk_e, n_used, slot0, slot1 = _routing(top_idx, batch, seq, lp)
    xg = _gather_rows(h, src)
    y = _ffn(xg, blk_e, n_used, od_moe_w_gate[0].astype(BF16), od_moe_w_up[0].astype(BF16),
             od_moe_w_down[0].astype(BF16), tm=MOE_ROWS)
    out = _combine(y, slot0, slot1, h, gates, vec(od_ln2_g[0]), vec(od_ln2_b[0]), batch, seq, lp)
    return out.reshape(batch, seq, D_MODEL)
```

```python
import functools

import jax
import jax.numpy as jnp
from jax import lax
from jax.experimental import pallas as pl
from jax.experimental.pallas import tpu as pltpu

F32 = jnp.float32
BF16 = jnp.bfloat16

D_MODEL = 2048
DEPTH = 2
N_META = 16
BLOCK = 128
SWA_HEADS = 16
SWA_KV_HEADS = 2
SWA_HEAD_DIM = 64
SWA_GROUP = SWA_HEADS // SWA_KV_HEADS
MLA_HEADS = 8
MLA_Q_RANK = 768
MLA_KV_RANK = 512
MLA_NOPE_DIM = 128
MLA_ROPE_DIM = 64
MLA_V_DIM = 128
MLA_QK_DIM = MLA_NOPE_DIM + MLA_ROPE_DIM
ROPE_THETA = 10000.0
SWA_Q_W = SWA_HEADS * SWA_HEAD_DIM
SWA_KV_W = SWA_KV_HEADS * SWA_HEAD_DIM
CONV_WIDTH = 31
D_FF = 7 * D_MODEL // 2
N_EXPERTS = 8
TOP_K = 2
DN_ALPHA = (2 * DEPTH) ** 0.25
NEG_INF = -1e30
LN_EPS = 1e-5
RMS_EPS = 1e-6

LANE = 128
SUBLANE = 8
VMEM_LIMIT = 56 * 1024 * 1024

ROW_TILE = 512
FRONT = ROW_TILE - N_META
MLA_PAD_W = 256
FF_TILE = 1024
MOE_ROWS = 512
GATHER_ROWS = 512
HALO = 32


def _cparams(sem):
    return pltpu.CompilerParams(dimension_semantics=sem, vmem_limit_bytes=VMEM_LIMIT)


def _layer_norm(x, g, b):
    mu = jnp.mean(x, axis=-1, keepdims=True)
    xc = x - mu
    var = jnp.mean(xc * xc, axis=-1, keepdims=True)
    return xc * lax.rsqrt(var + LN_EPS) * g + b


def _rms_norm(x, g):
    return x * lax.rsqrt(jnp.mean(x * x, axis=-1, keepdims=True) + RMS_EPS) * g


def _sigmoid(x):
    return 1.0 / (1.0 + jnp.exp(-x))


def _dot(a, b):
    return jnp.dot(a, b, preferred_element_type=F32)


def _dot_nt(a, b):
    return lax.dot_general(a, b, (((1,), (1,)), ((), ())), preferred_element_type=F32)


SWA_K_PAD_W = SWA_KV_HEADS * LANE
IN_ROW_W = SWA_K_PAD_W + MLA_Q_RANK + MLA_KV_RANK + 2 * LANE
IN_T_ROWS = SWA_Q_W + SWA_KV_W


def _inproj_kernel(x_ref, w_ref, wt_ref, cs_ref, aqt_ref, ak_ref, avt_ref, cq_ref, ckv_ref, kr_ref):
    xb = x_ref[...].astype(BF16)
    p = _dot(xb, w_ref[...])
    o = SWA_K_PAD_W
    ak_ref[...] = p[:, :o].astype(BF16)
    cq_ref[...] = p[:, o:o + MLA_Q_RANK]
    o += MLA_Q_RANK
    ckv_ref[...] = p[:, o:o + MLA_KV_RANK]
    o += MLA_KV_RANK
    kr = p[:, o:o + LANE]
    kr_rot = p[:, o + LANE:o + 2 * LANE]
    kr_ref[...] = (kr * cs_ref[:, :LANE] + kr_rot * cs_ref[:, LANE:]).astype(BF16)
    pt = _dot_nt(wt_ref[...], xb)
    aqt_ref[...] = pt[:SWA_Q_W, :].astype(BF16)
    avt_ref[...] = pt[SWA_Q_W:, :].astype(BF16)


def _inproj(h, w_row, w_t, cs, tm=256):
    tp = h.shape[0]
    row = lambda w: pl.BlockSpec((tm, w), lambda i: (i, 0))
    col = lambda r: pl.BlockSpec((r, tm), lambda i: (0, i))
    full = lambda a, b: pl.BlockSpec((a, b), lambda i: (0, 0))
    return pl.pallas_call(
        _inproj_kernel,
        grid=(tp // tm,),
        in_specs=[row(D_MODEL), full(D_MODEL, IN_ROW_W), full(IN_T_ROWS, D_MODEL), row(2 * LANE)],
        out_specs=[col(SWA_Q_W), row(SWA_K_PAD_W), col(SWA_KV_W),
                   row(MLA_Q_RANK), row(MLA_KV_RANK), row(LANE)],
        out_shape=[jax.ShapeDtypeStruct((SWA_Q_W, tp), BF16),
                   jax.ShapeDtypeStruct((tp, SWA_K_PAD_W), BF16),
                   jax.ShapeDtypeStruct((SWA_KV_W, tp), BF16),
                   jax.ShapeDtypeStruct((tp, MLA_Q_RANK), F32),
                   jax.ShapeDtypeStruct((tp, MLA_KV_RANK), F32),
                   jax.ShapeDtypeStruct((tp, LANE), BF16)],
        compiler_params=_cparams(("parallel",)),
        name="inproj",
    )(h, w_row, w_t, cs)


MLA_QT_ROWS = MLA_HEADS * MLA_PAD_W + MLA_HEADS * MLA_ROPE_DIM
LOG2E = 1.4426950408889634


def _mla_up_kernel(cq_ref, ckv_ref, kr_ref, cst_ref, gq_ref, gkv_ref, wqt_ref, wk_ref, wvt_ref,
                   qt_ref, kk_ref, vt_ref):
    scale = MLA_QK_DIM ** -0.5 * LOG2E
    tm = cq_ref.shape[0]
    cos = cst_ref[:MLA_ROPE_DIM, :]
    sin = cst_ref[MLA_ROPE_DIM:, :]
    cqn = _rms_norm(cq_ref[...], gq_ref[...]).astype(BF16)
    qa = _dot_nt(wqt_ref[...], cqn)
    rot0 = MLA_HEADS * MLA_PAD_W
    for h in range(MLA_HEADS):
        a = h * MLA_PAD_W
        r = a + MLA_NOPE_DIM
        qt_ref[a:r, :] = (qa[a:r, :] * scale).astype(BF16)
        roped = (qa[r:r + MLA_ROPE_DIM, :] * cos
                 + qa[rot0 + h * MLA_ROPE_DIM:rot0 + (h + 1) * MLA_ROPE_DIM, :] * sin)
        qt_ref[r:r + MLA_ROPE_DIM, :] = (roped * scale).astype(BF16)
        qt_ref[r + MLA_ROPE_DIM:a + MLA_PAD_W, :] = jnp.zeros((MLA_PAD_W - MLA_QK_DIM, tm), BF16)
    ckvn = _rms_norm(ckv_ref[...], gkv_ref[...]).astype(BF16)
    kn = _dot(ckvn, wk_ref[...])
    kr = kr_ref[...]
    for h in range(MLA_HEADS):
        kk_ref[:, h * MLA_PAD_W:h * MLA_PAD_W + LANE] = kn[:, h * MLA_NOPE_DIM:(h + 1) * MLA_NOPE_DIM].astype(BF16)
        kk_ref[:, h * MLA_PAD_W + LANE:(h + 1) * MLA_PAD_W] = kr
    vt_ref[0] = _dot_nt(wvt_ref[...], ckvn).astype(BF16)


def _mla_up(cq, ckv, kr, cst, gq, gkv, wqt, wk, wvt, tm=ROW_TILE):
    tp = cq.shape[0]
    row = lambda w: pl.BlockSpec((tm, w), lambda i: (i, 0))
    col = lambda r: pl.BlockSpec((r, tm), lambda i: (0, i))
    full = lambda a, b: pl.BlockSpec((a, b), lambda i: (0, 0))
    kw = MLA_HEADS * MLA_PAD_W
    vw = MLA_HEADS * MLA_V_DIM
    return pl.pallas_call(
        _mla_up_kernel,
        grid=(tp // tm,),
        in_specs=[row(MLA_Q_RANK), row(MLA_KV_RANK), row(LANE), col(2 * MLA_ROPE_DIM),
                  full(1, MLA_Q_RANK), full(1, MLA_KV_RANK),
                  full(MLA_QT_ROWS, MLA_Q_RANK), full(MLA_KV_RANK, MLA_HEADS * MLA_NOPE_DIM),
                  full(vw, MLA_KV_RANK)],
        out_specs=[col(kw), row(kw), pl.BlockSpec((1, vw, tm), lambda i: (i, 0, 0))],
        out_shape=[jax.ShapeDtypeStruct((kw, tp), BF16),
                   jax.ShapeDtypeStruct((tp, kw), BF16),
                   jax.ShapeDtypeStruct((tp // tm, vw, tm), BF16)],
        compiler_params=_cparams(("parallel",)),
        name="mla_up",
    )(cq, ckv, kr, cst, gq, gkv, wqt, wk, wvt)


META_BLOCK = FRONT // BLOCK
META_ROW0 = FRONT % BLOCK


def _swa_kernel(sink_ref, qt_ref, kc_ref, kp_ref, km_ref, vc_ref, vp_ref, vm_ref, o_ref):
    n = pl.program_id(1) - META_BLOCK
    r = lax.broadcasted_iota(jnp.int32, (3 * BLOCK, BLOCK), 0)
    i = lax.broadcasted_iota(jnp.int32, (3 * BLOCK, BLOCK), 1)
    never = 4 * BLOCK
    ok_meta = (r >= META_ROW0) & (r <= n * BLOCK + i)
    ok_prev = (r - BLOCK) > (i + jnp.where(n >= 2, 0, never))
    ok_cur = (r - 2 * BLOCK + jnp.where(n >= 1, 0, never)) <= i
    ok = ((r < BLOCK) & ok_meta) | ((r >= BLOCK) & (r < 2 * BLOCK) & ok_prev) | ((r >= 2 * BLOCK) & ok_cur)
    bias = jnp.where(ok, 0.0, NEG_INF)
    bias = jnp.concatenate([bias] * SWA_GROUP, axis=1)
    km, kp, kc = km_ref[...], kp_ref[...], kc_ref[...]
    outs = []
    for g in range(SWA_KV_HEADS):
        ks = slice(g * LANE, g * LANE + SWA_HEAD_DIM)
        ds = slice(g * SWA_HEAD_DIM, (g + 1) * SWA_HEAD_DIM)
        k = jnp.concatenate([km[:, ks], kp[:, ks], kc[:, ks]], axis=0)
        vt = jnp.concatenate([vm_ref[ds, :], vp_ref[ds, :], vc_ref[ds, :]], axis=1)
        qt = jnp.concatenate([qt_ref[(g * SWA_GROUP + hh) * SWA_HEAD_DIM:(g * SWA_GROUP + hh + 1) * SWA_HEAD_DIM, :]
                              for hh in range(SWA_GROUP)], axis=1)
        s = _dot(k, qt) + bias
        sink = sink_ref[g:g + 1, :]
        m = jnp.maximum(jnp.max(s, axis=0, keepdims=True), sink)
        p = jnp.exp2(s - m)
        den = jnp.sum(p, axis=0, keepdims=True) + jnp.exp2(sink - m)
        o = _dot(vt, p.astype(BF16)) / den
        outs += [o[:, hh * BLOCK:(hh + 1) * BLOCK] for hh in range(SWA_GROUP)]
    o_ref[0] = jnp.concatenate(outs, axis=0).T.astype(BF16)


def _swa(aqt, ak, avt, sink_rows, batch):
    tp = ak.shape[0]
    nb = tp // batch // BLOCK
    cur = lambda bi, n: bi * nb + n
    prev = lambda bi, n: bi * nb + jnp.maximum(n - 1, 0)
    meta = lambda bi, n: bi * nb + META_BLOCK
    kspec = lambda f: pl.BlockSpec((BLOCK, SWA_K_PAD_W), lambda bi, n: (f(bi, n), 0))
    vspec = lambda f: pl.BlockSpec((SWA_KV_W, BLOCK), lambda bi, n: (0, f(bi, n)))
    return pl.pallas_call(
        _swa_kernel,
        grid=(batch, nb),
        in_specs=[pl.BlockSpec((SWA_KV_HEADS, SWA_GROUP * BLOCK), lambda bi, n: (0, 0)),
                  pl.BlockSpec((SWA_Q_W, BLOCK), lambda bi, n: (0, cur(bi, n))),
                  kspec(cur), kspec(prev), kspec(meta), vspec(cur), vspec(prev), vspec(meta)],
        out_specs=pl.BlockSpec((1, BLOCK, SWA_Q_W), lambda bi, n: (bi, n, 0)),
        out_shape=jax.ShapeDtypeStruct((batch, nb * BLOCK, SWA_Q_W), BF16),
        compiler_params=_cparams(("parallel", "parallel")),
        name="swa",
    )(sink_rows, aqt, ak, ak, ak, avt, avt, avt)


def _mla_kernel(qt_ref, k_ref, vt_ref, o_ref, sa_scr, sb_scr, *, tq):
    qi = pl.program_id(2)
    qt = qt_ref[...]
    meta0 = FRONT // LANE * LANE

    def scores(ki):
        start = pl.multiple_of(ki * tq, tq)
        return _dot(k_ref[0, pl.ds(start, tq), :], qt)

    def masked(s, row0):
        kidx = row0 + lax.broadcasted_iota(jnp.int32, s.shape, 0)
        qidx = qi * tq + lax.broadcasted_iota(jnp.int32, s.shape, 1)
        return jnp.where((kidx <= qidx) & (kidx >= FRONT), s, NEG_INF)

    def update(s, smax, vb, carry):
        m, l, acc = carry
        m_new = jnp.maximum(m, smax)
        alpha = jnp.exp2(m - m_new)
        p = jnp.exp2(s - m_new)
        l = alpha * l + jnp.sum(p, axis=0, keepdims=True)
        acc = alpha * acc + _dot(vb, p.astype(BF16))
        return m_new, l, acc

    def update_masked(s, row0, vb, carry):
        s = masked(s, row0)
        return update(s, jnp.max(s, axis=0, keepdims=True), vb, carry)

    init = (jnp.full((1, tq), NEG_INF, F32), jnp.zeros((1, tq), F32), jnp.zeros((MLA_V_DIM, tq), F32))
    carry = update_masked(_dot(k_ref[0, meta0:tq, :], qt), meta0, vt_ref[0, :, meta0:tq], init)

    def issue(ki, dst):
        s = scores(ki)
        dst[...] = s
        return jnp.max(s, axis=0, keepdims=True)

    def consume(src, smax, ki, carry):
        return update(src[...], smax, vt_ref[ki], carry)

    def diagonal(src, carry):
        return update_masked(src[...], qi * tq, vt_ref[qi], carry)

    def tiles_after_first(carry):
        n_full = qi - 1

        def pair(t, c):
            smax_a, carry = c
            ki = 1 + 2 * t
            smax_b = issue(ki + 1, sb_scr)
            carry = consume(sa_scr, smax_a, ki, carry)
            smax_a = issue(ki + 2, sa_scr)
            carry = consume(sb_scr, smax_b, ki + 1, carry)
            return smax_a, carry

        smax_a, carry = lax.fori_loop(0, n_full // 2, pair, (issue(1, sa_scr), carry))

        def odd_tail(carry):
            issue(qi, sb_scr)
            return diagonal(sb_scr, consume(sa_scr, smax_a, qi - 1, carry))

        return lax.cond(n_full % 2 == 1, odd_tail, lambda c: diagonal(sa_scr, c), carry)

    _, l, acc = lax.cond(qi > 0, tiles_after_first, lambda c: c, carry)
    o_ref[0] = (acc / l).T.astype(BF16)


def _mla(qt, kk, vt, batch, tq=ROW_TILE):
    lp = kk.shape[1]
    nq = lp // tq
    return pl.pallas_call(
        functools.partial(_mla_kernel, tq=tq),
        grid=(batch, MLA_HEADS, nq),
        in_specs=[pl.BlockSpec((MLA_PAD_W, tq), lambda bi, h, i: (h, bi * nq + i)),
                  pl.BlockSpec((1, lp, MLA_PAD_W), lambda bi, h, i: (bi, 0, h)),
                  pl.BlockSpec((nq, MLA_V_DIM, tq), lambda bi, h, i: (bi, h, 0))],
        out_specs=pl.BlockSpec((1, tq, MLA_V_DIM), lambda bi, h, i: (bi, i, h)),
        out_shape=jax.ShapeDtypeStruct((batch, lp, MLA_HEADS * MLA_V_DIM), BF16),
        scratch_shapes=[pltpu.VMEM((tq, tq), F32), pltpu.VMEM((tq, tq), F32)],
        compiler_params=_cparams(("parallel", "parallel", "arbitrary")),
        name="mla",
    )(qt, kk, vt)


def _outproj_kernel(oa_ref, ob_ref, h_ref, w_ref, g_ref, b_ref, o_ref):
    mix = _dot(oa_ref[...], w_ref[:SWA_Q_W, :]) + _dot(ob_ref[...], w_ref[SWA_Q_W:, :])
    o_ref[...] = _layer_norm(DN_ALPHA * h_ref[...] + mix, g_ref[...], b_ref[...])


def _outproj(oa, ob, h, w_o, g, b, tm=ROW_TILE):
    tp = h.shape[0]
    row = lambda w: pl.BlockSpec((tm, w), lambda i: (i, 0))
    full = lambda a, c: pl.BlockSpec((a, c), lambda i: (0, 0))
    return pl.pallas_call(
        _outproj_kernel,
        grid=(tp // tm,),
        in_specs=[row(SWA_Q_W), row(MLA_HEADS * MLA_V_DIM), row(D_MODEL),
                  full(D_MODEL, D_MODEL), full(1, D_MODEL), full(1, D_MODEL)],
        out_specs=row(D_MODEL),
        out_shape=jax.ShapeDtypeStruct((tp, D_MODEL), F32),
        compiler_params=_cparams(("parallel",)),
        name="outproj_ln",
    )(oa, ob, h, w_o, g, b)


def _swiglu_step(xb, wg_ref, wu_ref, wd_ref, acc_scr):
    a = _dot(xb, wg_ref[0])
    u = _dot(xb, wu_ref[0])
    act = (a * _sigmoid(a) * u).astype(BF16)
    acc_scr[...] += _dot(act, wd_ref[0])


def _ffn_dense_kernel(be_ref, nu_ref, x_ref, wg_ref, wu_ref, wd_ref, g_ref, b_ref, o_ref, xb_scr, acc_scr):
    j = pl.program_id(1)

    @pl.when(j == 0)
    def _():
        xb_scr[...] = x_ref[...].astype(BF16)
        acc_scr[...] = jnp.zeros_like(acc_scr)

    _swiglu_step(xb_scr[...], wg_ref, wu_ref, wd_ref, acc_scr)

    @pl.when(j == pl.num_programs(1) - 1)
    def _():
        o_ref[...] = _layer_norm(DN_ALPHA * x_ref[...] + acc_scr[...], g_ref[...], b_ref[...])


def _ffn_routed_kernel(be_ref, nu_ref, x_ref, wg_ref, wu_ref, wd_ref, o_ref, acc_scr):
    j = pl.program_id(1)

    @pl.when(pl.program_id(0) < nu_ref[0])
    def _():
        @pl.when(j == 0)
        def _():
            acc_scr[...] = jnp.zeros_like(acc_scr)

        _swiglu_step(x_ref[...], wg_ref, wu_ref, wd_ref, acc_scr)

        @pl.when(j == pl.num_programs(1) - 1)
        def _():
            o_ref[...] = acc_scr[...]

    @pl.when((pl.program_id(0) >= nu_ref[0]) & (j == pl.num_programs(1) - 1))
    def _():
        o_ref[...] = jnp.zeros_like(o_ref)


def _ffn(x, block_expert, n_used, w_gate, w_up, w_down, ln=None, tm=ROW_TILE, tf=FF_TILE):
    rows = x.shape[0]
    nj = D_FF // tf
    blk = lambda i, nu: jnp.minimum(i, nu[0] - 1)
    ff = lambda i, j, nu: jnp.where(i < nu[0], j, nj - 1)
    row = pl.BlockSpec((tm, D_MODEL), lambda i, j, be, nu: (blk(i, nu), 0))
    in_specs = [row,
                pl.BlockSpec((1, D_MODEL, tf), lambda i, j, be, nu: (be[blk(i, nu)], 0, ff(i, j, nu))),
                pl.BlockSpec((1, D_MODEL, tf), lambda i, j, be, nu: (be[blk(i, nu)], 0, ff(i, j, nu))),
                pl.BlockSpec((1, tf, D_MODEL), lambda i, j, be, nu: (be[blk(i, nu)], ff(i, j, nu), 0))]
    scratch = [pltpu.VMEM((tm, D_MODEL), F32)]
    args = [block_expert, n_used, x, w_gate, w_up, w_down]
    if ln is not None:
        vec = pl.BlockSpec((1, D_MODEL), lambda i, j, be, nu: (0, 0))
        in_specs += [vec, vec]
        scratch = [pltpu.VMEM((tm, D_MODEL), BF16)] + scratch
        args += list(ln)
        body, name = _ffn_dense_kernel, "ffn_dense_ln"
    else:
        body, name = _ffn_routed_kernel, "ffn_routed"
    return pl.pallas_call(
        body,
        grid_spec=pltpu.PrefetchScalarGridSpec(
            num_scalar_prefetch=2,
            grid=(rows // tm, nj),
            in_specs=in_specs,
            out_specs=pl.BlockSpec((tm, D_MODEL), lambda i, j, be, nu: (i, 0)),
            scratch_shapes=scratch),
        out_shape=jax.ShapeDtypeStruct((rows, D_MODEL), F32),
        compiler_params=_cparams(("arbitrary", "arbitrary")),
        name=name,
    )(*args)


def _pw1_kernel(h_ref, wa_ref, wg_ref, ba_ref, bg_ref, u_ref, xb_scr):
    @pl.when(pl.program_id(1) == 0)
    def _():
        xb_scr[...] = h_ref[...].astype(BF16)

    xb = xb_scr[...]
    a = _dot(xb, wa_ref[...]) + ba_ref[...]
    g = _dot(xb, wg_ref[...]) + bg_ref[...]
    u_ref[...] = a * _sigmoid(g)


def _pw1(h, w_pw1, b_pw1, tm=ROW_TILE, tn=1024):
    tp = h.shape[0]
    nj = D_MODEL // tn
    return pl.pallas_call(
        _pw1_kernel,
        grid=(tp // tm, nj),
        in_specs=[pl.BlockSpec((tm, D_MODEL), lambda i, j: (i, 0)),
                  pl.BlockSpec((D_MODEL, tn), lambda i, j: (0, j)),
                  pl.BlockSpec((D_MODEL, tn), lambda i, j: (0, j + nj)),
                  pl.BlockSpec((1, tn), lambda i, j: (0, j)),
                  pl.BlockSpec((1, tn), lambda i, j: (0, j + nj))],
        out_specs=pl.BlockSpec((tm, tn), lambda i, j: (i, j)),
        out_shape=jax.ShapeDtypeStruct((tp, D_MODEL), F32),
        scratch_shapes=[pltpu.VMEM((tm, D_MODEL), BF16)],
        compiler_params=_cparams(("parallel", "arbitrary")),
        name="pw1_glu",
    )(h, w_pw1, w_pw1, b_pw1, b_pw1)


CONV_ROWS = 64
CONV_COLS = 128


def _conv_kernel(uc_ref, up_ref, h_ref, wdw_ref, bdw_ref, gcn_ref, bcn_ref, w2_ref, b2_ref,
                 g1_ref, b1_ref, wr_ref, br_ref, ho_ref, idx_ref, gate_ref, win_scr, cv_scr,
                 *, tm, tiles_per_batch):
    row0 = (pl.program_id(0) % tiles_per_batch) * tm
    win_scr[HALO:, :] = uc_ref[...]
    win_scr[:HALO, :] = up_ref[...]

    @pl.when(row0 < FRONT)
    def _():
        r = lax.broadcasted_iota(jnp.int32, (tm + HALO, 1), 0) + (row0 - HALO)
        win_scr[...] = jnp.where(r >= FRONT, win_scr[...], 0.0)

    off = HALO - (CONV_WIDTH - 1)

    def rows_body(rc, _):
        r0 = pl.multiple_of(rc * CONV_ROWS, CONV_ROWS)
        for cc in range(D_MODEL // CONV_COLS):
            cs = slice(cc * CONV_COLS, (cc + 1) * CONV_COLS)
            blk = win_scr[pl.ds(r0, CONV_ROWS + HALO), cs]
            acc = jnp.zeros((CONV_ROWS, CONV_COLS), F32)
            for b in range(SUBLANE):
                ub = pltpu.roll(blk, CONV_ROWS + HALO - (off + b), axis=0)
                for a in range((CONV_WIDTH - 1 - b) // SUBLANE + 1):
                    k = SUBLANE * a + b
                    acc = acc + ub[SUBLANE * a:SUBLANE * a + CONV_ROWS, :] * wdw_ref[k:k + 1, cs]
            cv_scr[pl.ds(r0, CONV_ROWS), cs] = acc
        return 0

    lax.fori_loop(0, tm // CONV_ROWS, rows_body, 0)

    c = _layer_norm(cv_scr[...] + bdw_ref[...], gcn_ref[...], bcn_ref[...])
    c = c * _sigmoid(c)
    mix = _dot(c.astype(BF16), w2_ref[...]) + b2_ref[...]
    hn = _layer_norm(DN_ALPHA * h_ref[...] + mix, g1_ref[...], b1_ref[...])
    ho_ref[...] = hn

    lg = [jnp.sum(hn * wr_ref[e:e + 1, :], axis=-1, keepdims=True) + br_ref[e] for e in range(N_EXPERTS)]

    def top1(vals):
        best = vals[0]
        for v in vals[1:]:
            best = jnp.maximum(best, v)
        idx = jnp.full(best.shape, N_EXPERTS - 1, jnp.int32)
        for e in range(N_EXPERTS - 2, -1, -1):
            idx = jnp.where(vals[e] == best, e, idx)
        return best, idx

    m1, i1 = top1(lg)
    m2, i2 = top1([jnp.where(i1 == e, -jnp.inf, lg[e]) for e in range(N_EXPERTS)])
    lane = lax.broadcasted_iota(jnp.int32, (hn.shape[0], LANE), 1)
    e = jnp.exp(m2 - m1)
    gate1 = 1.0 / (1.0 + e)
    gate2 = e * gate1
    idx_ref[...] = jnp.where(lane == 0, i1, jnp.where(lane == 1, i2, 0))
    gate_ref[...] = jnp.where(lane == 0, gate1, jnp.where(lane == 1, gate2, 0.0))


def _conv_module(u, h, wdw, bdw, gcn, bcn, w2, b2, g1, b1, wr, br, tiles_per_batch, tm=256):
    tp = h.shape[0]
    row = lambda w: pl.BlockSpec((tm, w), lambda i: (i, 0))
    full = lambda a, c: pl.BlockSpec((a, c), lambda i: (0, 0))
    halo_blocks = tm // HALO
    return pl.pallas_call(
        functools.partial(_conv_kernel, tm=tm, tiles_per_batch=tiles_per_batch * (ROW_TILE // tm)),
        grid=(tp // tm,),
        in_specs=[row(D_MODEL),
                  pl.BlockSpec((HALO, D_MODEL), lambda i: (jnp.maximum(i * halo_blocks - 1, 0), 0)),
                  row(D_MODEL),
                  full(HALO, D_MODEL), full(1, D_MODEL), full(1, D_MODEL), full(1, D_MODEL),
                  full(D_MODEL, D_MODEL), full(1, D_MODEL), full(1, D_MODEL), full(1, D_MODEL),
                  full(N_EXPERTS, D_MODEL), pl.BlockSpec(memory_space=pltpu.SMEM)],
        out_specs=[row(D_MODEL), row(LANE), row(LANE)],
        out_shape=[jax.ShapeDtypeStruct((tp, D_MODEL), F32),
                   jax.ShapeDtypeStruct((tp, LANE), jnp.int32),
                   jax.ShapeDtypeStruct((tp, LANE), F32)],
        scratch_shapes=[pltpu.VMEM((tm + HALO, D_MODEL), F32), pltpu.VMEM((tm, D_MODEL), F32)],
        compiler_params=_cparams(("parallel",)),
        name="conv_module_ln_router",
    )(u, u, h, wdw, bdw, gcn, bcn, w2, b2, g1, b1, wr, br)


def _row_copy(src_hbm, row, buf, slot, sem):
    return pltpu.make_async_copy(src_hbm.at[pl.ds(row, 1), :], buf.at[pl.ds(slot, 1), :], sem)


def _issue_rows(src_hbm, idx_ref, buf, sem, rows):
    def body(r, _):
        _row_copy(src_hbm, idx_ref[0, 0, r], buf, r, sem).start()
        return 0

    lax.fori_loop(0, rows, body, 0, unroll=8)


def _drain_rows(src_hbm, buf, sem, rows):
    def body(r, _):
        _row_copy(src_hbm, 0, buf, r, sem).wait()
        return 0

    lax.fori_loop(0, rows, body, 0, unroll=8)


def _gather_kernel(idx_ref, nxt_ref, src_hbm, o_ref, buf, sem, *, rows):
    i = pl.program_id(0)
    slot = i % 2

    @pl.when(i == 0)
    def _():
        _issue_rows(src_hbm, idx_ref, buf.at[0], sem.at[0], rows)

    @pl.when(i + 1 < pl.num_programs(0))
    def _():
        _issue_rows(src_hbm, nxt_ref, buf.at[1 - slot], sem.at[1 - slot], rows)

    _drain_rows(src_hbm, buf.at[slot], sem.at[slot], rows)
    o_ref[...] = buf[slot].astype(o_ref.dtype)


def _gather_rows(src, idx, rows=GATHER_ROWS):
    n = idx.shape[0]
    steps = n // rows
    islot = lambda f: pl.BlockSpec((1, 1, rows), lambda i: (f(i), 0, 0), memory_space=pltpu.SMEM)
    idx3 = idx.reshape(steps, 1, rows)
    return pl.pallas_call(
        functools.partial(_gather_kernel, rows=rows),
        grid=(steps,),
        in_specs=[islot(lambda i: i), islot(lambda i: jnp.minimum(i + 1, steps - 1)),
                  pl.BlockSpec(memory_space=pl.ANY)],
        out_specs=pl.BlockSpec((rows, D_MODEL), lambda i: (i, 0)),
        out_shape=jax.ShapeDtypeStruct((n, D_MODEL), BF16),
        scratch_shapes=[pltpu.VMEM((2, rows, D_MODEL), F32), pltpu.SemaphoreType.DMA((2,))],
        compiler_params=_cparams(("arbitrary",)),
        name="dispatch_gather",
    )(idx3, idx3, src)


def _combine_kernel(i0_ref, i1_ref, n0_ref, n1_ref, y_hbm, h_ref, gate_ref, g_ref, b_ref, o_ref,
                    buf0, buf1, sem, *, rows):
    t = pl.program_id(0) * pl.num_programs(1) + pl.program_id(1)
    steps = pl.num_programs(0) * pl.num_programs(1)
    slot = t % 2

    def issue(a_ref, b_ref, s):
        _issue_rows(y_hbm, a_ref, buf0.at[s], sem.at[s], rows)
        _issue_rows(y_hbm, b_ref, buf1.at[s], sem.at[s], rows)

    @pl.when(t == 0)
    def _():
        issue(i0_ref, i1_ref, 0)

    @pl.when(t + 1 < steps)
    def _():
        issue(n0_ref, n1_ref, 1 - slot)

    _drain_rows(y_hbm, buf0.at[slot], sem.at[slot], rows)
    _drain_rows(y_hbm, buf1.at[slot], sem.at[slot], rows)
    gate = gate_ref[...]
    y = buf0[slot] * gate[:, 0:1] + buf1[slot] * gate[:, 1:2]
    o_ref[...] = _layer_norm(DN_ALPHA * h_ref[...] + y, g_ref[...], b_ref[...])


def _combine(y, slot0, slot1, h, gate, g, b, batch, seq, lp, rows=GATHER_ROWS):
    per_batch = seq // rows
    lead = ROW_TILE // rows
    lp_blocks = lp // rows
    steps = batch * per_batch
    hrow = lambda w: pl.BlockSpec((rows, w), lambda bi, j: (bi * lp_blocks + lead + j, 0))
    cur = lambda bi, j: bi * per_batch + j
    nxt = lambda bi, j: jnp.minimum(bi * per_batch + j + 1, steps - 1)
    islot = lambda f: pl.BlockSpec((1, 1, rows), lambda bi, j: (f(bi, j), 0, 0), memory_space=pltpu.SMEM)
    vec = pl.BlockSpec((1, D_MODEL), lambda bi, j: (0, 0))
    s0 = slot0.reshape(steps, 1, rows)
    s1 = slot1.reshape(steps, 1, rows)
    return pl.pallas_call(
        functools.partial(_combine_kernel, rows=rows),
        grid=(batch, per_batch),
        in_specs=[islot(cur), islot(cur), islot(nxt), islot(nxt),
                  pl.BlockSpec(memory_space=pl.ANY), hrow(D_MODEL), hrow(LANE), vec, vec],
        out_specs=pl.BlockSpec((rows, D_MODEL), lambda bi, j: (cur(bi, j), 0)),
        out_shape=jax.ShapeDtypeStruct((batch * seq, D_MODEL), F32),
        scratch_shapes=[pltpu.VMEM((2, rows, D_MODEL), F32), pltpu.VMEM((2, rows, D_MODEL), F32),
                        pltpu.SemaphoreType.DMA((2,))],
        compiler_params=_cparams(("arbitrary", "arbitrary")),
        name="combine_ln",
    )(s0, s1, s0, s1, y, h, gate, g, b)


def _rot_cols(w):
    half = MLA_ROPE_DIM // 2
    return jnp.concatenate([-w[..., half:], w[..., :half]], axis=-1)


def _rope_table(batch, lp):
    half = MLA_ROPE_DIM // 2
    inv = ROPE_THETA ** (-jnp.arange(half, dtype=F32) * 2.0 / MLA_ROPE_DIM)
    pos = (jnp.arange(lp) - FRONT).astype(F32)
    ang = pos[:, None] * inv[None, :]
    z = jnp.zeros((lp, LANE - MLA_ROPE_DIM), F32)
    cos = jnp.concatenate([jnp.cos(ang), jnp.cos(ang), z], axis=-1)
    sin = jnp.concatenate([jnp.sin(ang), jnp.sin(ang), z], axis=-1)
    return jnp.tile(jnp.concatenate([cos, sin], axis=-1), (batch, 1))


def _pad_cols(w, width):
    return jnp.pad(w, ((0, 0), (0, width - w.shape[-1])))


def _routing(top_idx, batch, seq, lp):
    t = batch * seq
    a = t * TOP_K
    real = top_idx.reshape(batch, lp, LANE)[:, ROW_TILE:, :TOP_K]
    flat_e = real.reshape(a)
    tok_row = (jnp.arange(batch)[:, None] * lp + ROW_TILE + jnp.arange(seq)[None, :]).reshape(t)
    flat_row = jnp.repeat(tok_row, TOP_K).astype(jnp.int32)
    onehot = (flat_e[:, None] == jnp.arange(N_EXPERTS)[None, :]).astype(jnp.int32)
    csum = jnp.cumsum(onehot, axis=0)
    rank = jnp.take_along_axis(csum, flat_e[:, None], axis=1)[:, 0] - 1
    counts = csum[-1]
    padded = (counts + MOE_ROWS - 1) // MOE_ROWS * MOE_ROWS
    pend = jnp.cumsum(padded)
    pstart = pend - padded
    dest = (pstart[flat_e] + rank).astype(jnp.int32)
    nblk = a // MOE_ROWS + N_EXPERTS
    p = nblk * MOE_ROWS
    src = jnp.zeros((p,), jnp.int32).at[dest].set(flat_row)
    blk_e = jnp.minimum(jnp.sum(jnp.arange(nblk)[:, None] * MOE_ROWS >= pend[None, :], -1),
                        N_EXPERTS - 1).astype(jnp.int32)
    n_used = (pend[-1:] // MOE_ROWS).astype(jnp.int32)
    dest2 = dest.reshape(t, TOP_K)
    return src, blk_e, n_used, dest2[:, 0], dest2[:, 1]


def kernel(x, meta_tokens, ev_w_in, ev_sinks, ev_g_cq, ev_w_uq, ev_g_ckv, ev_w_ukv, ev_w_o, ev_ln1_g, ev_ln1_b, ev_ffn_w_gate, ev_ffn_w_up, ev_ffn_w_down, ev_ln2_g, ev_ln2_b, od_w_pw1, od_b_pw1, od_w_dw, od_b_dw, od_g_cn, od_b_cn, od_w_pw2, od_b_pw2, od_ln1_g, od_ln1_b, od_w_router, od_b_router, od_moe_w_gate, od_moe_w_up, od_moe_w_down, od_ln2_g, od_ln2_b):
    batch, seq, _ = x.shape
    assert seq % ROW_TILE == 0
    lp = ROW_TILE + seq
    tp = batch * lp
    vec = lambda v: v.reshape(1, -1).astype(F32)

    meta = jnp.broadcast_to(meta_tokens[None].astype(x.dtype), (batch, N_META, D_MODEL))
    h = jnp.concatenate([jnp.zeros((batch, FRONT, D_MODEL), x.dtype), meta, x], axis=1).reshape(tp, D_MODEL)
    cs = _rope_table(batch, lp)

    w_in = ev_w_in[0]
    o = SWA_Q_W + 2 * SWA_KV_W + MLA_Q_RANK + MLA_KV_RANK
    w_kr = w_in[:, o:]
    w_ak = w_in[:, SWA_Q_W:SWA_Q_W + SWA_KV_W].reshape(D_MODEL, SWA_KV_HEADS, SWA_HEAD_DIM)
    w_ak = jnp.pad(w_ak, ((0, 0), (0, 0), (0, LANE - SWA_HEAD_DIM))).reshape(D_MODEL, SWA_K_PAD_W)
    w_row = jnp.concatenate([w_ak, w_in[:, SWA_Q_W + 2 * SWA_KV_W:o],
                             _pad_cols(w_kr, LANE), _pad_cols(_rot_cols(w_kr), LANE)], axis=1).astype(BF16)
    w_t = jnp.concatenate([w_in[:, :SWA_Q_W] * (SWA_HEAD_DIM ** -0.5 * LOG2E),
                           w_in[:, SWA_Q_W + SWA_KV_W:SWA_Q_W + 2 * SWA_KV_W]], axis=1).T.astype(BF16)
    aqt, ak, avt, cq, ckv, kr = _inproj(h, w_row, w_t, cs)

    w_uq = ev_w_uq[0].reshape(MLA_Q_RANK, MLA_HEADS, MLA_QK_DIM)
    w_uq_pad = jnp.pad(w_uq, ((0, 0), (0, 0), (0, MLA_PAD_W - MLA_QK_DIM))).reshape(MLA_Q_RANK, -1)
    w_uq_rot = _rot_cols(w_uq[..., MLA_NOPE_DIM:]).reshape(MLA_Q_RANK, -1)
    wqt = jnp.concatenate([w_uq_pad, w_uq_rot], axis=1).T.astype(BF16)
    w_ukv = ev_w_ukv[0].reshape(MLA_KV_RANK, MLA_HEADS, MLA_NOPE_DIM + MLA_V_DIM)
    wk = w_ukv[..., :MLA_NOPE_DIM].reshape(MLA_KV_RANK, -1).astype(BF16)
    wvt = w_ukv[..., MLA_NOPE_DIM:].reshape(MLA_KV_RANK, -1).T.astype(BF16)
    cst = jnp.concatenate([cs[:, :MLA_ROPE_DIM], cs[:, LANE:LANE + MLA_ROPE_DIM]], axis=1).T
    qt, kk, vt = _mla_up(cq, ckv, kr, cst, vec(ev_g_cq[0]), vec(ev_g_ckv[0]), wqt, wk, wvt)

    sink_rows = jnp.repeat(ev_sinks[0].astype(F32) * LOG2E, BLOCK).reshape(SWA_KV_HEADS, SWA_GROUP * BLOCK)
    out_a = _swa(aqt, ak, avt, sink_rows, batch)
    out_b = _mla(qt, kk.reshape(batch, lp, -1), vt, batch)
    h = _outproj(out_a.reshape(tp, -1), out_b.reshape(tp, -1), h, ev_w_o[0].astype(BF16),
                 vec(ev_ln1_g[0]), vec(ev_ln1_b[0]))

    n_dense = tp // ROW_TILE
    h = _ffn(h, jnp.zeros((n_dense,), jnp.int32), jnp.full((1,), n_dense, jnp.int32),
             ev_ffn_w_gate.astype(BF16), ev_ffn_w_up.astype(BF16),
             ev_ffn_w_down.astype(BF16), ln=(vec(ev_ln2_g[0]), vec(ev_ln2_b[0])))

    u = _pw1(h, od_w_pw1[0].astype(BF16), vec(od_b_pw1[0]))
    wdw = jnp.pad(od_w_dw[0].astype(F32), ((0, HALO - CONV_WIDTH), (0, 0)))
    h, top_idx, gates = _conv_module(
        u, h, wdw, vec(od_b_dw[0]), vec(od_g_cn[0]), vec(od_b_cn[0]),
        od_w_pw2[0].astype(BF16), vec(od_b_pw2[0]), vec(od_ln1_g[0]), vec(od_ln1_b[0]),
        od_w_router[0].astype(F32).T, od_b_router[0].astype(F32),
        tiles_per_batch=lp // ROW_TILE)

    src, blk_e, n_used, slot0, slot1 = _routing(top_idx, batch, seq, lp)
    xg = _gather_rows(h, src)
    y = _ffn(xg, blk_e, n_used, od_moe_w_gate[0].astype(BF16), od_moe_w_up[0].astype(BF16),
             od_moe_w_down[0].astype(BF16), tm=MOE_ROWS)
    out = _combine(y, slot0, slot1, h, gates, vec(od_ln2_g[0]), vec(od_ln2_b[0]), batch, seq, lp)
    return out.reshape(batch, seq, D_MODEL)
```
